```python
import jax, jax.numpy as jnp
from jax import lax
import numpy as np

D_MODEL = 2048
BATCH = 16
SEQ = 256
DEPTH = 4
DEC_BATCH = 8
DEC_SEQ = 1024
PAST_LEN = 512

GRID_W = 64
N_EVEN = (DEPTH + 1) // 2
N_ODD = DEPTH // 2
RET_WIDTH = D_MODEL // 2
N_RET_HEADS = 8
RET_DK = RET_WIDTH // N_RET_HEADS
RET_DV = RET_DK
HG_WIDTH = D_MODEL // 2
N_HG_HEADS = 8
HG_DK = HG_WIDTH // N_HG_HEADS
HG_DV = HG_DK
MIX_WIDTH = RET_WIDTH + HG_WIDTH
EVEN_SPLITS = (RET_WIDTH, RET_WIDTH, RET_WIDTH, RET_WIDTH, HG_WIDTH, HG_WIDTH, HG_WIDTH, HG_WIDTH, HG_WIDTH)
EVEN_IN = 4 * RET_WIDTH + 5 * HG_WIDTH
CHUNK = 64
ROPE_BASE = 10000.0
ROPE_FREQS = RET_DK // 4
D_RNN = D_MODEL
N_RG_BLOCKS = 16
RG_BLOCK = D_RNN // N_RG_BLOCKS
CONV_W = 4
CONV_LEFT = (CONV_W - 1) // 2
RG_C = 8.0
N_GROUPS = 4
EXPERTS_PER_GROUP = 4
N_EXPERTS = N_GROUPS * EXPERTS_PER_GROUP
TOP_K_INNER = 2
D_EXPERT = 512
EPS = 1e-6
F_MIN = 1e-20

kernel_name = "hybrid_retention_hgrn2_rglru_hmoe_diffusion_step"


def rms_norm(x, gain):
    xf = x.astype(jnp.float32)
    y = xf * lax.rsqrt(jnp.mean(xf * xf, axis=-1, keepdims=True) + EPS)
    return (y * gain.astype(jnp.float32)).astype(x.dtype)


def flip_t(a):
    return jnp.flip(a, axis=1)


def adaln(cond, w, b):
    m = jax.nn.silu(cond) @ w + b
    return jnp.split(m[:, None, :], 6, axis=-1)


def apply_rope(x, cos, sin):
    half = x.shape[-1] // 2
    xf = x.astype(jnp.float32)
    x1, x2 = xf[..., :half], xf[..., half:]
    c, s = cos[None, :, None, :], sin[None, :, None, :]
    return jnp.concatenate([x1 * c - x2 * s, x1 * s + x2 * c], axis=-1).astype(x.dtype)


def to_chunks(a):
    B, T, H, d = a.shape
    return a.reshape(B, T // CHUNK, CHUNK, H, d).transpose(1, 0, 3, 2, 4)


def from_chunks(a):
    n, B, H, C, d = a.shape
    return a.transpose(1, 0, 3, 2, 4).reshape(B, n * C, H, d)


def retention_chunked(q, k, v, log_gamma, s0):
    f32 = jnp.float32
    qc, kc, vc = to_chunks(q.astype(f32)), to_chunks(k.astype(f32)), to_chunks(v.astype(f32))
    pos = jnp.arange(CHUNK, dtype=f32)
    lg = log_gamma.astype(f32)
    rel = pos[:, None] - pos[None, :]
    decay = jnp.where(rel >= 0, jnp.exp(lg[:, None, None] * jnp.maximum(rel, 0.0)), 0.0)
    q_dec = jnp.exp(lg[:, None] * (pos + 1.0))
    k_dec = jnp.exp(lg[:, None] * (CHUNK - 1.0 - pos))
    chunk_dec = jnp.exp(lg * CHUNK)

    def step(S, inp):
        qi, ki, vi = inp
        scores = jnp.einsum('bhtd,bhsd->bhts', qi, ki) * decay
        o = (jnp.einsum('bhts,bhsv->bhtv', scores, vi)
             + jnp.einsum('bhtd,bhdv->bhtv', qi, S) * q_dec[None, :, :, None])
        S = S * chunk_dec[None, :, None, None] + jnp.einsum('bhsd,bhsv->bhdv', ki * k_dec[None, :, :, None], vi)
        return S, o

    S, o = lax.scan(step, s0.astype(f32), (qc, kc, vc))
    return from_chunks(o), S


def gated_chunked(q, k, v, log_f, s0):
    f32 = jnp.float32
    qc, kc, vc, gc = (to_chunks(q.astype(f32)), to_chunks(k.astype(f32)),
                      to_chunks(v.astype(f32)), to_chunks(log_f.astype(f32)))
    mask = jnp.tril(jnp.ones((CHUNK, CHUNK), dtype=bool))[:, :, None]

    def step(S, inp):
        qi, ki, vi, gi = inp
        b = jnp.cumsum(gi, axis=2)
        diff = b[:, :, :, None, :] - b[:, :, None, :, :]
        dec = jnp.where(mask, jnp.exp(jnp.minimum(diff, 0.0)), 0.0)
        scores = jnp.einsum('bhtd,bhsd,bhtsd->bhts', qi, ki, dec)
        o = (jnp.einsum('bhts,bhsv->bhtv', scores, vi)
             + jnp.einsum('bhtd,bhdv->bhtv', qi * jnp.exp(b), S))
        b_last = b[:, :, -1:, :]
        S = (S * jnp.exp(b_last)[:, :, 0, :, None]
             + jnp.einsum('bhsd,bhsv->bhdv', ki * jnp.exp(b_last - b), vi))
        return S, o

    S, o = lax.scan(step, s0.astype(f32), (qc, kc, vc, gc))
    return from_chunks(o), S


def linear_scan(a, b, h0):
    def comb(l, r):
        return (l[0] * r[0], r[0] * l[1] + r[1])
    A, Bc = lax.associative_scan(comb, (a, b), axis=1)
    h = A * h0.astype(jnp.float32)[:, None, :] + Bc
    return h, h[:, -1]


def even_mixer(h, w_in, w_out, ret_decay, lb, g_ret, g_hg, rope, s_ret, s_hg):
    B, T, _ = h.shape
    f32 = jnp.float32
    proj = h @ w_in
    rq, rk, rv, rg, hq, hff, hfb, hv, hg = jnp.split(proj, list(np.cumsum(EVEN_SPLITS)[:-1]), axis=-1)
    rq = rq.reshape(B, T, N_RET_HEADS, RET_DK)
    rk = rk.reshape(B, T, N_RET_HEADS, RET_DK)
    rv = rv.reshape(B, T, N_RET_HEADS, RET_DV)
    if rope is not None:
        rq, rk = apply_rope(rq, *rope), apply_rope(rk, *rope)
    rk = rk * (RET_DK ** -0.5)
    log_gamma = -jnp.exp(ret_decay.astype(f32))
    ro_f, sr_f = retention_chunked(rq, rk, rv, log_gamma[0], s_ret[:, 0])
    ro_b, sr_b = retention_chunked(flip_t(rq), flip_t(rk), flip_t(rv), log_gamma[1], s_ret[:, 1])
    ro = (ro_f + flip_t(ro_b)).astype(h.dtype)
    ro = rms_norm(ro, g_ret.reshape(N_RET_HEADS, RET_DV)) * jax.nn.silu(rg.reshape(B, T, N_RET_HEADS, RET_DV))
    hq = jax.nn.silu(hq).reshape(B, T, N_HG_HEADS, HG_DK)
    hv = hv.reshape(B, T, N_HG_HEADS, HG_DV)
    lbh = lb.astype(f32).reshape(N_HG_HEADS, HG_DK)

    def forget(z):
        z = z.astype(f32).reshape(B, T, N_HG_HEADS, HG_DK)
        sg = jax.nn.sigmoid(z)
        f = lbh + (1.0 - lbh) * sg
        log_f = jnp.log(jnp.maximum(f, F_MIN))
        k = (1.0 - lbh) * (1.0 - sg)
        return log_f, k

    lf_f, k_f = forget(hff)
    lf_b, k_b = forget(hfb)
    ho_f, sh_f = gated_chunked(hq, k_f, hv, lf_f, s_hg[:, 0])
    ho_b, sh_b = gated_chunked(flip_t(hq), flip_t(k_b), flip_t(hv), flip_t(lf_b), s_hg[:, 1])
    ho = (ho_f + flip_t(ho_b)).astype(h.dtype)
    ho = rms_norm(ho, g_hg.reshape(N_HG_HEADS, HG_DV)) * jax.nn.silu(hg.reshape(B, T, N_HG_HEADS, HG_DV))
    y = jnp.concatenate([ro.reshape(B, T, RET_WIDTH), ho.reshape(B, T, HG_WIDTH)], axis=-1) @ w_out
    return y, jnp.stack([sr_f, sr_b], axis=1), jnp.stack([sh_f, sh_b], axis=1)


def odd_mixer(h, w_in, conv_w, conv_b, w_a, b_a, w_x, b_x, lam, w_out, s0):
    B, T, _ = h.shape
    f32 = jnp.float32
    gate_in, xb = jnp.split(h @ w_in, 2, axis=-1)
    gate = jax.nn.gelu(gate_in)
    xp = jnp.pad(xb, ((0, 0), (CONV_LEFT, CONV_W - 1 - CONV_LEFT), (0, 0)))
    xc = conv_b + sum(xp[:, j:j + T] * conv_w[j] for j in range(CONV_W))
    xcf = xc.astype(f32)
    xblk = xcf.reshape(B, T, N_RG_BLOCKS, RG_BLOCK)

    def gates(d):
        r = jax.nn.sigmoid(jnp.einsum('btnk,nkj->btnj', xblk, w_a[d].astype(f32)).reshape(B, T, D_RNN) + b_a[d].astype(f32))
        i = jax.nn.sigmoid(jnp.einsum('btnk,nkj->btnj', xblk, w_x[d].astype(f32)).reshape(B, T, D_RNN) + b_x[d].astype(f32))
        log_a = -RG_C * r * jax.nn.softplus(-lam[d].astype(f32))
        a = jnp.exp(log_a)
        bt = jnp.sqrt(jnp.maximum(-jnp.expm1(2.0 * log_a), F_MIN)) * (i * xcf)
        return a, bt

    a_f, b_f = gates(0)
    h_f, s_f = linear_scan(a_f, b_f, s0[:, 0])
    a_b, b_b = gates(1)
    h_b, s_b = linear_scan(flip_t(a_b), flip_t(b_b), s0[:, 1])
    y = (h_f + flip_t(h_b)).astype(h.dtype) * gate
    return y @ w_out, jnp.stack([s_f, s_b], axis=1)


def hier_moe(h, w_group, b_group, w_router, b_router, w1, w3, w2):
    B, T, D = h.shape
    f32 = jnp.float32
    xf = h.reshape(B * T, D)
    g_logits = (xf @ w_group + b_group).astype(f32)
    g_sel = jnp.argmax(g_logits, axis=-1)
    p_grp = jnp.max(jax.nn.softmax(g_logits, axis=-1), axis=-1)
    grp_onehot = jax.nn.one_hot(g_sel, N_GROUPS, dtype=f32)
    e_logits = (xf @ w_router + b_router).astype(f32).reshape(-1, N_GROUPS, EXPERTS_PER_GROUP)
    e_in = jnp.einsum('ng,nge->ne', grp_onehot, e_logits)
    top_v, top_i = lax.top_k(e_in, TOP_K_INNER)
    top_w = jax.nn.softmax(top_v, axis=-1) * p_grp[:, None]
    e_idx = g_sel[:, None] * EXPERTS_PER_GROUP + top_i
    gates = jnp.einsum('nk,nke->ne', top_w, jax.nn.one_hot(e_idx, N_EXPERTS, dtype=f32))
    hid = jax.nn.silu(jnp.einsum('nd,edf->nef', xf, w1)) * jnp.einsum('nd,edf->nef', xf, w3)
    out = jnp.einsum('nef,ne,efd->nd', hid, gates.astype(hid.dtype), w2)
    return out.reshape(B, T, D).astype(h.dtype)


def setup_inputs(seed: int = 0) -> dict:
    key = jax.random.key(seed)
    keys = jax.random.split(key, 40)
    f32 = jnp.float32

    def nrm(i, shape, scale):
        return jax.random.normal(keys[i], shape, f32) * scale

    D = D_MODEL
    base_decay = jnp.log(-jnp.log1p(-(2.0 ** (-5.0 - jnp.arange(N_RET_HEADS, dtype=f32)))))
    a0 = jax.random.uniform(keys[30], (N_ODD, 2, D_RNN), f32, 0.9, 0.999)
    s = a0 ** (1.0 / RG_C)
    return {
        "x_prompt": nrm(0, (BATCH, SEQ, D), 1.0),
        "x_sample": nrm(1, (DEC_BATCH, DEC_SEQ, D), 1.0),
        "state_ret": nrm(2, (DEC_BATCH, N_EVEN, 2, N_RET_HEADS, RET_DK, RET_DV), 1.0),
        "state_hgrn": nrm(3, (DEC_BATCH, N_EVEN, 2, N_HG_HEADS, HG_DK, HG_DV), 1.0),
        "state_rglru": nrm(4, (DEC_BATCH, N_ODD, 2, D_RNN), 0.5),
        "c": nrm(5, (DEC_BATCH, D), 1.0),
        "c_ctx": nrm(6, (D,), 1.0),
        "w_ada": nrm(7, (DEPTH, D, 6 * D), 0.5 * D ** -0.5),
        "b_ada": nrm(8, (DEPTH, 6 * D), 0.02),
        "g_norm1": 1.0 + nrm(9, (DEPTH, D), 0.02),
        "g_norm2": 1.0 + nrm(10, (DEPTH, D), 0.02),
        "w_even_in": nrm(11, (N_EVEN, D, EVEN_IN), D ** -0.5),
        "w_even_out": nrm(12, (N_EVEN, MIX_WIDTH, D), MIX_WIDTH ** -0.5),
        "ret_decay": base_decay + nrm(13, (N_EVEN, 2, N_RET_HEADS), 0.05),
        "hg_lb_logits": nrm(14, (N_EVEN, HG_WIDTH), 0.5),
        "g_ret_head": 1.0 + nrm(15, (N_EVEN, RET_WIDTH), 0.02),
        "g_hg_head": 1.0 + nrm(16, (N_EVEN, HG_WIDTH), 0.02),
        "w_odd_in": nrm(17, (N_ODD, D, 2 * D_RNN), D ** -0.5),
        "conv_w": nrm(18, (N_ODD, CONV_W, D_RNN), 0.5),
        "conv_b": nrm(19, (N_ODD, D_RNN), 0.02),
        "w_a": nrm(20, (N_ODD, 2, N_RG_BLOCKS, RG_BLOCK, RG_BLOCK), RG_BLOCK ** -0.5),
        "b_a": nrm(21, (N_ODD, 2, D_RNN), 0.02),
        "w_x": nrm(22, (N_ODD, 2, N_RG_BLOCKS, RG_BLOCK, RG_BLOCK), RG_BLOCK ** -0.5),
        "b_x": nrm(23, (N_ODD, 2, D_RNN), 0.02),
        "rg_lambda": jnp.log(s) - jnp.log1p(-s),
        "w_odd_out": nrm(24, (N_ODD, D_RNN, D), D_RNN ** -0.5),
        "w_group": nrm(25, (DEPTH, D, N_GROUPS), D ** -0.5),
        "b_group": nrm(26, (DEPTH, N_GROUPS), 0.01),
        "w_router": nrm(27, (DEPTH, D, N_EXPERTS), D ** -0.5),
        "b_router": nrm(28, (DEPTH, N_EXPERTS), 0.01),
        "w1": nrm(29, (DEPTH, N_EXPERTS, D, D_EXPERT), D ** -0.5),
        "w3": nrm(31, (DEPTH, N_EXPERTS, D, D_EXPERT), D ** -0.5),
        "w2": nrm(32, (DEPTH, N_EXPERTS, D_EXPERT, D), D_EXPERT ** -0.5),
        "g_final": 1.0 + nrm(33, (D,), 0.02),
    }


def reference(x_prompt, x_sample, state_ret, state_hgrn, state_rglru, c, c_ctx,
              w_ada, b_ada, g_norm1, g_norm2, w_even_in, w_even_out, ret_decay, hg_lb_logits,
              g_ret_head, g_hg_head, w_odd_in, conv_w, conv_b, w_a, b_a, w_x, b_x, rg_lambda,
              w_odd_out, w_group, b_group, w_router, b_router, w1, w3, w2, g_final):
    f32 = jnp.float32
    lb_sm = jax.nn.softmax(hg_lb_logits.astype(f32), axis=0)
    lb_all = jnp.cumsum(lb_sm, axis=0) - lb_sm[0:1]
    t_lat = x_sample.shape[1]
    rows_n = t_lat // GRID_W
    t_idx = jnp.arange(rows_n * GRID_W)
    rows = (t_idx // GRID_W).astype(f32)
    cols = (t_idx % GRID_W).astype(f32)
    freqs = ROPE_BASE ** (-jnp.arange(ROPE_FREQS, dtype=f32) / ROPE_FREQS)
    ang = jnp.concatenate([rows[:, None] * freqs, cols[:, None] * freqs], axis=-1)
    rope = (jnp.cos(ang), jnp.sin(ang))
    ctx_cond = c_ctx[None, :]
    bp = x_prompt.shape[0]

    def run_layer(x, cond, l, s_a, s_b, rp):
        sh1, sc1, g1, sh2, sc2, g2 = adaln(cond, w_ada[l], b_ada[l])
        h = rms_norm(x, g_norm1[l]) * (1.0 + sc1) + sh1
        if l % 2 == 0:
            e = l // 2
            y, sa, sb = even_mixer(h, w_even_in[e], w_even_out[e], ret_decay[e], lb_all[e],
                                   g_ret_head[e], g_hg_head[e], rp, s_a, s_b)
        else:
            o = l // 2
            y, sa = odd_mixer(h, w_odd_in[o], conv_w[o], conv_b[o], w_a[o], b_a[o], w_x[o], b_x[o],
                              rg_lambda[o], w_odd_out[o], s_a)
            sb = None
        x = x + g1 * y
        h = rms_norm(x, g_norm2[l]) * (1.0 + sc2) + sh2
        x = x + g2 * hier_moe(h, w_group[l], b_group[l], w_router[l], b_router[l], w1[l], w3[l], w2[l])
        return x, sa, sb

    xp, xs = x_prompt, x_sample
    new_ret, new_hg, new_rg = [], [], []
    for l in range(DEPTH):
        if l % 2 == 0:
            e = l // 2
            zr = jnp.zeros((bp, 2, N_RET_HEADS, RET_DK, RET_DV), f32)
            zh = jnp.zeros((bp, 2, N_HG_HEADS, HG_DK, HG_DV), f32)
            xp, sr, sh = run_layer(xp, ctx_cond, l, zr, zh, None)
            xs, _, _ = run_layer(xs, c, l, state_ret[:, e], state_hgrn[:, e], rope)
            new_ret.append(sr)
            new_hg.append(sh)
        else:
            o = l // 2
            zg = jnp.zeros((bp, 2, D_RNN), f32)
            xp, sg, _ = run_layer(xp, ctx_cond, l, zg, None, None)
            xs, _, _ = run_layer(xs, c, l, state_rglru[:, o], None, rope)
            new_rg.append(sg)

    y_prompt = rms_norm(xp, g_final)
    y_sample = rms_norm(xs, g_final)
    new_state_ret = jnp.stack(new_ret, axis=1)
    new_state_hgrn = jnp.stack(new_hg, axis=1)
    new_state_rglru = jnp.stack(new_rg, axis=1)
    return (y_prompt, y_sample, new_state_ret, new_state_hgrn, new_state_rglru)
```

```python
import functools

import jax
import jax.numpy as jnp
from jax import lax
from jax.experimental import pallas as pl
from jax.experimental.pallas import tpu as pltpu

F32 = jnp.float32
BF16 = jnp.bfloat16
HIGHEST = lax.Precision.HIGHEST

D_MODEL = 2048
GRID_W = 64
HEAD_DIM = 128
N_HEADS = 8
MIX_HALF = N_HEADS * HEAD_DIM
EVEN_IN = 9 * MIX_HALF
ROPE_BASE = 10000.0
RG_BLOCK = 128
RG_C = 8.0
N_GROUPS = 4
EXPERTS_PER_GROUP = 4
N_EXPERTS = N_GROUPS * EXPERTS_PER_GROUP
D_EXPERT = 512
EPS = 1e-6
F_MIN = 1e-20
COND_ROWS = 16
LANES = 128
SUBLANES = 8

RET_CHUNK = 128
HG_CHUNK = 64
HG_SUB = 16
ODD_CB = 512
MOE_TM = 256
VMEM_LIMIT = 52 * 1024 * 1024


def _cp(n_axes, vmem=VMEM_LIMIT):
    return pltpu.CompilerParams(dimension_semantics=("arbitrary",) * n_axes, vmem_limit_bytes=vmem)


def _silu(x):
    return x * jax.nn.sigmoid(x)


def _dot(a, b, **kw):
    return jnp.dot(a, b, preferred_element_type=F32, **kw)


def _dot_nt(a, b):
    return lax.dot_general(a, b, (((1,), (1,)), ((), ())), preferred_element_type=F32)


def _dot_tn(a, b):
    return lax.dot_general(a, b, (((0,), (0,)), ((), ())), preferred_element_type=F32)


def _norm_mod(x, gain, scale, shift):
    ms = jnp.mean(x * x, axis=-1, keepdims=True)
    return x * lax.rsqrt(ms + EPS) * gain * (1.0 + scale) + shift


def _ada_kernel(cond_ref, w_ref, b_ref, o_ref):
    o_ref[...] = _dot(_silu(cond_ref[...]), w_ref[...], precision=HIGHEST) + b_ref[...]


def _adaln_all(cond, w_ada, b_ada):
    depth, d, n6 = w_ada.shape
    tn = 1024
    out = pl.pallas_call(
        _ada_kernel,
        grid=(depth, n6 // tn),
        in_specs=[pl.BlockSpec((COND_ROWS, d), lambda l, j: (0, 0)),
                  pl.BlockSpec((None, d, tn), lambda l, j: (l, 0, j)),
                  pl.BlockSpec((None, 1, tn), lambda l, j: (l, 0, j))],
        out_specs=pl.BlockSpec((None, COND_ROWS, tn), lambda l, j: (l, 0, j)),
        out_shape=jax.ShapeDtypeStruct((depth, COND_ROWS, n6), F32),
        compiler_params=_cp(2),
    )(cond, w_ada, b_ada.reshape(depth, 1, n6))
    return out.reshape(depth, COND_ROWS, 6, 1, d)


class _Rows:
    def __init__(self, bp, tp, bs, ts):
        self.bp, self.tp, self.bs, self.ts = bp, tp, bs, ts
        self.n_ctx = bp * tp
        self.n = self.n_ctx + bs * ts
        self.ctx_row = bs

    def cond_row(self, i, tm):
        n_ctx_tiles = self.n_ctx // tm
        per_seq = self.ts // tm
        return jnp.where(i < n_ctx_tiles, self.ctx_row, (i - n_ctx_tiles) // per_seq)

    def tile(self, cap):
        tm = min(cap, self.ts)
        assert self.ts % tm == 0 and self.n_ctx % tm == 0
        return tm


def _mod_spec(rows, layer, which, tm):
    d = D_MODEL
    return pl.BlockSpec((None, None, None, 1, d),
                        lambda i, *_: (layer, rows.cond_row(i, tm), which, 0, 0))


def _norm0_kernel(x_ref, g_ref, sh_ref, sc_ref, h_ref):
    h_ref[...] = _norm_mod(x_ref[...], g_ref[...], sc_ref[...], sh_ref[...]).astype(BF16)


def _norm0(x, g_norm, mod, rows, layer):
    n, d = x.shape
    tm = rows.tile(512)
    return pl.pallas_call(
        _norm0_kernel,
        grid=(n // tm,),
        in_specs=[pl.BlockSpec((tm, d), lambda i: (i, 0)),
                  pl.BlockSpec((None, 1, d), lambda i: (layer, 0, 0)),
                  _mod_spec(rows, layer, 0, tm),
                  _mod_spec(rows, layer, 1, tm)],
        out_specs=pl.BlockSpec((tm, d), lambda i: (i, 0)),
        out_shape=jax.ShapeDtypeStruct((n, d), BF16),
        compiler_params=_cp(1),
    )(x, g_norm.reshape(-1, 1, d), mod, mod)


def _matmul_kernel(h_ref, w_ref, o_ref):
    o_ref[...] = _dot(h_ref[...], w_ref[...])


def _in_proj(h, w):
    n, k = h.shape
    _, nout = w.shape
    tm = 1024 if n % 1024 == 0 else 128
    tn = 512
    return pl.pallas_call(
        _matmul_kernel,
        grid=(n // tm, nout // tn),
        in_specs=[pl.BlockSpec((tm, k), lambda i, j: (i, 0)),
                  pl.BlockSpec((k, tn), lambda i, j: (0, j))],
        out_specs=pl.BlockSpec((tm, tn), lambda i, j: (i, j)),
        out_shape=jax.ShapeDtypeStruct((n, nout), F32),
        compiler_params=_cp(2),
    )(h, w)


def _ret_kernel(*refs, t, rope, has_state, emit_state):
    refs = list(refs)
    lg_ref = refs.pop(0)
    q_ref, k_ref, v_ref, g_ref, gain_ref = [refs.pop(0) for _ in range(5)]
    if rope:
        cos_ref, sin_ref = refs.pop(0), refs.pop(0)
    if has_state:
        s0_ref = refs.pop(0)
    refs.pop(0)
    o_ref = refs.pop(0)
    if emit_state:
        st_ref = refs.pop(0)
    sb_scr, qs, ks = refs

    c = min(RET_CHUNK, t)
    n_chunks = t // c
    head = pl.program_id(1)
    lgf = lg_ref[0, head]
    lgb = lg_ref[1, head]

    q = q_ref[...]
    k = k_ref[...]
    if rope:
        cs, sn = cos_ref[...], sin_ref[...]
        q = q * cs + pltpu.roll(q, HEAD_DIM // 2, 1) * sn
        k = k * cs + pltpu.roll(k, HEAD_DIM // 2, 1) * sn
    qs[...] = q
    ks[...] = k * (HEAD_DIM ** -0.5)

    pos_c = lax.broadcasted_iota(jnp.int32, (c, 1), 0).astype(F32)
    pos_r = lax.broadcasted_iota(jnp.int32, (1, c), 1).astype(F32)
    rel = pos_c - pos_r
    decay = (jnp.where(rel >= 0, jnp.exp(lgf * jnp.maximum(rel, 0.0)), 0.0)
             + jnp.where(rel <= 0, jnp.exp(lgb * jnp.maximum(-rel, 0.0)), 0.0))
    qdf = jnp.exp(lgf * (pos_c + 1.0))
    kdf = jnp.exp(lgf * (c - 1.0 - pos_c))
    qdb = jnp.exp(lgb * (c - pos_c))
    kdb = jnp.exp(lgb * pos_c)
    full = jnp.full((1, HEAD_DIM), float(c), F32)
    cdf = jnp.exp(lgf * full)
    cdb = jnp.exp(lgb * full)

    def rows_of(i):
        return pl.ds(pl.multiple_of(i * c, c), c)

    def bwd_body(ii, s):
        i = n_chunks - 1 - ii
        sb_scr[i] = s
        sl = rows_of(i)
        kv = _dot_tn((ks[sl, :] * kdb).astype(BF16), v_ref[sl, :].astype(BF16))
        return s * cdb + kv

    s0_b = s0_ref[1] if has_state else jnp.zeros((HEAD_DIM, HEAD_DIM), F32)
    s_b = lax.fori_loop(0, n_chunks, bwd_body, s0_b)

    gain = gain_ref[...]

    def fwd_body(i, s):
        sl = rows_of(i)
        qc, kc = qs[sl, :], ks[sl, :]
        vb = v_ref[sl, :].astype(BF16)
        scores = _dot_nt(qc.astype(BF16), kc.astype(BF16)) * decay
        o = (_dot(scores.astype(BF16), vb)
             + _dot((qc * qdf).astype(BF16), s.astype(BF16))
             + _dot((qc * qdb).astype(BF16), sb_scr[i].astype(BF16)))
        ms = jnp.mean(o * o, axis=-1, keepdims=True)
        o = o * lax.rsqrt(ms + EPS) * gain
        o_ref[sl, :] = (o * _silu(g_ref[sl, :])).astype(BF16)
        return s * cdf + _dot_tn((kc * kdf).astype(BF16), vb)

    s0_f = s0_ref[0] if has_state else jnp.zeros((HEAD_DIM, HEAD_DIM), F32)
    s_f = lax.fori_loop(0, n_chunks, fwd_body, s0_f)
    if emit_state:
        st_ref[0] = s_f
        st_ref[1] = s_b


def _retention(proj, y_mix, log_gamma, g_head, row0, b, t, rope_tabs, s0, emit_state):
    rb0 = row0 // t
    blk = lambda col0: pl.BlockSpec((t, HEAD_DIM), lambda bi, h: (rb0 + bi, col0 + h))
    in_specs = [pl.BlockSpec(memory_space=pltpu.SMEM),
                blk(0), blk(N_HEADS), blk(2 * N_HEADS), blk(3 * N_HEADS),
                pl.BlockSpec((None, 1, HEAD_DIM), lambda bi, h: (h, 0, 0))]
    args = [log_gamma, proj, proj, proj, proj, g_head.reshape(N_HEADS, 1, HEAD_DIM)]
    if rope_tabs is not None:
        in_specs += [pl.BlockSpec((t, HEAD_DIM), lambda bi, h: (0, 0))] * 2
        args += list(rope_tabs)
    if s0 is not None:
        in_specs.append(pl.BlockSpec((None, 2, None, HEAD_DIM, HEAD_DIM), lambda bi, h: (bi, 0, h, 0, 0)))
        args.append(s0)
    in_specs.append(pl.BlockSpec(memory_space=pl.ANY))
    args.append(y_mix)
    alias_idx = len(args) - 1
    out_specs = [pl.BlockSpec((t, HEAD_DIM), lambda bi, h: (rb0 + bi, h))]
    out_shape = [jax.ShapeDtypeStruct(y_mix.shape, y_mix.dtype)]
    if emit_state:
        out_specs.append(pl.BlockSpec((None, 2, None, HEAD_DIM, HEAD_DIM), lambda bi, h: (bi, 0, h, 0, 0)))
        out_shape.append(jax.ShapeDtypeStruct((b, 2, N_HEADS, HEAD_DIM, HEAD_DIM), F32))
    n_chunks = t // min(RET_CHUNK, t)
    outs = pl.pallas_call(
        functools.partial(_ret_kernel, t=t, rope=rope_tabs is not None, has_state=s0 is not None,
                          emit_state=emit_state),
        grid=(b, N_HEADS),
        in_specs=in_specs,
        out_specs=out_specs,
        out_shape=out_shape,
        scratch_shapes=[pltpu.VMEM((n_chunks, HEAD_DIM, HEAD_DIM), F32),
                        pltpu.VMEM((t, HEAD_DIM), F32),
                        pltpu.VMEM((t, HEAD_DIM), F32)],
        input_output_aliases={alias_idx: 0},
        compiler_params=_cp(2),
    )(*args)
    return outs[0], (outs[1] if emit_state else None)


def _hg_scores(q, k, g, xs_ref, row0, fwd):
    c = q.shape[0]
    nb = c // HG_SUB
    g3 = g.reshape(nb, HG_SUB, HEAD_DIM)
    k3 = k.reshape(nb, HG_SUB, HEAD_DIM)
    q3 = q.reshape(nb, HG_SUB, HEAD_DIM)
    for s in range(HG_SUB):
        e = jnp.exp(jnp.minimum(g3 - g3[:, s:s + 1, :], 0.0))
        x = (q3 * e) * k3[:, s:s + 1, :]
        xs_ref[pl.ds(row0, c), pl.ds(s * HEAD_DIM, HEAD_DIM)] = x.reshape(c, HEAD_DIM).astype(BF16)

    row = lax.broadcasted_iota(jnp.int32, (c, c), 0)
    col = lax.broadcasted_iota(jnp.int32, (c, c), 1)
    acc = jnp.zeros((c, c), F32)
    w = HG_SUB
    while w < c:
        nb2 = c // (2 * w)
        g4 = g.reshape(nb2, 2 * w, HEAD_DIM)
        ref = g4[:, w - 1:w, :] if fwd else g4[:, w:w + 1, :]
        qt = q.reshape(nb2, 2 * w, HEAD_DIM) * jnp.exp(jnp.minimum(g4 - ref, 0.0))
        kt = k.reshape(nb2, 2 * w, HEAD_DIM) * jnp.exp(jnp.minimum(ref - g4, 0.0))
        sc = _dot_nt(qt.reshape(c, HEAD_DIM).astype(BF16), kt.reshape(c, HEAD_DIM).astype(BF16))
        same = (row // (2 * w)) == (col // (2 * w))
        r_hi = (row % (2 * w)) >= w
        c_hi = (col % (2 * w)) >= w
        mask = same & ((r_hi & ~c_hi) if fwd else (~r_hi & c_hi))
        acc = acc + jnp.where(mask, sc, 0.0)
        w *= 2
    return acc


def _hg_kernel(*refs, t, has_state, emit_state):
    refs = list(refs)
    q_ref, ff_ref, fb_ref, v_ref, gate_ref, lb_ref, gain_ref, bsel_ref = [refs.pop(0) for _ in range(8)]
    if has_state:
        s0_ref = refs.pop(0)
    refs.pop(0)
    o_ref = refs.pop(0)
    if emit_state:
        st_ref = refs.pop(0)
    sb_scr, qa, kf, lf, kb, lb_, xs = refs

    c = min(HG_CHUNK, t)
    n_chunks = t // c
    lbh = lb_ref[...]
    qa[...] = _silu(q_ref[...])

    def forget(z_ref, k_out, l_out):
        sg = jax.nn.sigmoid(z_ref[...])
        l_out[...] = jnp.log(jnp.maximum(lbh + (1.0 - lbh) * sg, F_MIN))
        k_out[...] = (1.0 - lbh) * (1.0 - sg)

    forget(ff_ref, kf, lf)
    forget(fb_ref, kb, lb_)

    row = lax.broadcasted_iota(jnp.int32, (c, c), 0)
    col = lax.broadcasted_iota(jnp.int32, (c, c), 1)
    lower = (row >= col)
    upper = (row <= col)
    lower_f = lower.astype(F32)
    upper_f = upper.astype(F32)
    same_sub = (row // HG_SUB) == (col // HG_SUB)

    def rows_of(i):
        return pl.ds(pl.multiple_of(i * c, c), c)

    def bwd_body(ii, s):
        i = n_chunks - 1 - ii
        sb_scr[i] = s
        sl = rows_of(i)
        a = _dot(upper_f, lb_[sl, :], precision=HIGHEST)
        a0 = a[0:1, :]
        kd = kb[sl, :] * jnp.exp(a0 - a)
        return s * jnp.exp(a0) + _dot_tn(v_ref[sl, :].astype(BF16), kd.astype(BF16))

    s0_b = s0_ref[1].T if has_state else jnp.zeros((HEAD_DIM, HEAD_DIM), F32)
    s_b = lax.fori_loop(0, n_chunks, bwd_body, s0_b)

    gain = gain_ref[...]

    def fwd_body(i, s):
        sl = rows_of(i)
        q = qa[sl, :]
        vb = v_ref[sl, :].astype(BF16)
        kfc, kbc = kf[sl, :], kb[sl, :]
        gf = _dot(lower_f, lf[sl, :], precision=HIGHEST)
        gb = _dot(upper_f, lb_[sl, :], precision=HIGHEST)
        off_f = _hg_scores(q, kfc, gf, xs, 0, True)
        off_b = _hg_scores(q, kbc, gb, xs, c, False)
        diag = _dot(xs[...], bsel_ref[...])
        p = (off_f + off_b
             + jnp.where(same_sub & lower, diag[:c], 0.0)
             + jnp.where(same_sub & upper, diag[c:], 0.0))
        o = (_dot(p.astype(BF16), vb)
             + _dot_nt((q * jnp.exp(gf)).astype(BF16), s.astype(BF16))
             + _dot_nt((q * jnp.exp(gb)).astype(BF16), sb_scr[i].astype(BF16)))
        ms = jnp.mean(o * o, axis=-1, keepdims=True)
        o = o * lax.rsqrt(ms + EPS) * gain
        o_ref[sl, :] = (o * _silu(gate_ref[sl, :])).astype(BF16)
        g_end = gf[c - 1:c, :]
        kd = kfc * jnp.exp(g_end - gf)
        return s * jnp.exp(g_end) + _dot_tn(vb, kd.astype(BF16))

    s0_f = s0_ref[0].T if has_state else jnp.zeros((HEAD_DIM, HEAD_DIM), F32)
    s_f = lax.fori_loop(0, n_chunks, fwd_body, s0_f)
    if emit_state:
        st_ref[0] = s_f.T
        st_ref[1] = s_b.T


def _hgrn2(proj, y_mix, lb, g_head, row0, b, t, s0, emit_state):
    rb0 = row0 // t
    c = min(HG_CHUNK, t)
    n_chunks = t // c
    blk = lambda col0: pl.BlockSpec((t, HEAD_DIM), lambda bi, h: (rb0 + bi, col0 + h))
    per_head = pl.BlockSpec((None, 1, HEAD_DIM), lambda bi, h: (h, 0, 0))
    j_of_row = jnp.arange(HG_SUB * HEAD_DIM) // HEAD_DIM
    bsel = (j_of_row[:, None] == (jnp.arange(c) % HG_SUB)[None, :]).astype(BF16)
    in_specs = [blk(4 * N_HEADS), blk(5 * N_HEADS), blk(6 * N_HEADS), blk(7 * N_HEADS), blk(8 * N_HEADS),
                per_head, per_head,
                pl.BlockSpec((HG_SUB * HEAD_DIM, c), lambda bi, h: (0, 0))]
    args = [proj] * 5 + [lb.reshape(N_HEADS, 1, HEAD_DIM), g_head.reshape(N_HEADS, 1, HEAD_DIM), bsel]
    if s0 is not None:
        in_specs.append(pl.BlockSpec((None, 2, None, HEAD_DIM, HEAD_DIM), lambda bi, h: (bi, 0, h, 0, 0)))
        args.append(s0)
    in_specs.append(pl.BlockSpec(memory_space=pl.ANY))
    args.append(y_mix)
    alias_idx = len(args) - 1
    out_specs = [pl.BlockSpec((t, HEAD_DIM), lambda bi, h: (rb0 + bi, N_HEADS + h))]
    out_shape = [jax.ShapeDtypeStruct(y_mix.shape, y_mix.dtype)]
    if emit_state:
        out_specs.append(pl.BlockSpec((None, 2, None, HEAD_DIM, HEAD_DIM), lambda bi, h: (bi, 0, h, 0, 0)))
        out_shape.append(jax.ShapeDtypeStruct((b, 2, N_HEADS, HEAD_DIM, HEAD_DIM), F32))
    seq = pltpu.VMEM((t, HEAD_DIM), F32)
    outs = pl.pallas_call(
        functools.partial(_hg_kernel, t=t, has_state=s0 is not None, emit_state=emit_state),
        grid=(b, N_HEADS),
        in_specs=in_specs,
        out_specs=out_specs,
        out_shape=out_shape,
        scratch_shapes=[pltpu.VMEM((n_chunks, HEAD_DIM, HEAD_DIM), F32), seq, seq, seq, seq, seq,
                        pltpu.VMEM((2 * c, HG_SUB * HEAD_DIM), BF16)],
        input_output_aliases={alias_idx: 0},
        compiler_params=_cp(2),
    )(*args)
    return outs[0], (outs[1] if emit_state else None)


def _block_scan(a, b, reverse):
    row = lax.broadcasted_iota(jnp.int32, a.shape, 0)
    k = 1
    while k < SUBLANES:
        shift = (SUBLANES - k) if reverse else k
        valid = (row < SUBLANES - k) if reverse else (row >= k)
        a_prev = pltpu.roll(a, shift, 0)
        b_prev = pltpu.roll(b, shift, 0)
        b = jnp.where(valid, a * b_prev + b, b)
        a = jnp.where(valid, a * a_prev, a)
        k *= 2
    return a, b


def _odd_kernel(*refs, t, has_state, emit_state):
    refs = list(refs)
    (gi_ref, xb_ref, cw_ref, cb_ref, wa_ref, wx_ref, ba_ref, bx_ref, sp_ref) = [refs.pop(0) for _ in range(9)]
    if has_state:
        s0_ref = refs.pop(0)
    refs.pop(0)
    o_ref = refs.pop(0)
    if emit_state:
        st_ref = refs.pop(0)
    xpad, xc, a_f, b_f, a_b, b_b, h_f = refs

    cb = xb_ref.shape[1]
    pad = SUBLANES
    zeros = jnp.zeros((pad, cb), F32)
    xpad[pl.ds(0, pad), :] = zeros
    xpad[pl.ds(pad + t, pad), :] = zeros
    xpad[pl.ds(pad, t), :] = xb_ref[...]
    for n in range(cb // RG_BLOCK):
        cols = pl.ds(n * RG_BLOCK, RG_BLOCK)
        xc[:, cols] = (cb_ref[:, cols]
                       + xpad[pl.ds(pad - 1, t), cols] * cw_ref[0:1, cols]
                       + xpad[pl.ds(pad, t), cols] * cw_ref[1:2, cols]
                       + xpad[pl.ds(pad + 1, t), cols] * cw_ref[2:3, cols]
                       + xpad[pl.ds(pad + 2, t), cols] * cw_ref[3:4, cols])

    rc = min(256, t)

    def gate_body(i, carry):
        sl = pl.ds(pl.multiple_of(i * rc, rc), rc)
        for n in range(cb // RG_BLOCK):
            cols = pl.ds(n * RG_BLOCK, RG_BLOCK)
            x_blk = xc[sl, cols]
            x_bf = x_blk.astype(BF16)
            for d, (a_out, b_out) in enumerate(((a_f, b_f), (a_b, b_b))):
                r = jax.nn.sigmoid(_dot(x_bf, wa_ref[d, n]) + ba_ref[d, :, cols])
                gate_i = jax.nn.sigmoid(_dot(x_bf, wx_ref[d, n]) + bx_ref[d, :, cols])
                log_a = -RG_C * r * sp_ref[d, :, cols]
                a = jnp.exp(log_a)
                a_out[sl, cols] = a
                one_minus_a2 = -jnp.tanh(log_a) * (1.0 + a * a)
                b_out[sl, cols] = jnp.sqrt(jnp.maximum(one_minus_a2, F_MIN)) * (gate_i * x_blk)
        return carry

    lax.fori_loop(0, t // rc, gate_body, 0)

    n_blocks = t // SUBLANES

    def rows_of(j):
        return pl.ds(pl.multiple_of(j * SUBLANES, SUBLANES), SUBLANES)

    def fwd_body(j, h_prev):
        sl = rows_of(j)
        a, b = _block_scan(a_f[sl, :], b_f[sl, :], False)
        h = a * h_prev + b
        h_f[sl, :] = h
        return jnp.broadcast_to(h[SUBLANES - 1:SUBLANES, :], h.shape)

    h0_f = s0_ref[0] if has_state else jnp.zeros((1, cb), F32)
    last_f = lax.fori_loop(0, n_blocks, fwd_body, jnp.broadcast_to(h0_f, (SUBLANES, cb)))

    def bwd_body(jj, h_next):
        sl = rows_of(n_blocks - 1 - jj)
        a, b = _block_scan(a_b[sl, :], b_b[sl, :], True)
        h = a * h_next + b
        gate = jax.nn.gelu(gi_ref[sl, :], approximate=True)
        o_ref[sl, :] = ((h_f[sl, :] + h) * gate).astype(BF16)
        return jnp.broadcast_to(h[0:1, :], h.shape)

    h0_b = s0_ref[1] if has_state else jnp.zeros((1, cb), F32)
    first_b = lax.fori_loop(0, n_blocks, bwd_body, jnp.broadcast_to(h0_b, (SUBLANES, cb)))
    if emit_state:
        st_ref[0] = last_f[0:1, :]
        st_ref[1] = first_b[0:1, :]


def _rglru(proj, y_mix, conv_w, conv_b, w_a, w_x, b_a, b_x, softplus_neg_lam, row0, b, t, s0, emit_state):
    d = D_MODEL
    cb = ODD_CB
    ncb = d // cb
    nrb = cb // RG_BLOCK
    rb0 = row0 // t
    vec = lambda rows: pl.BlockSpec((rows, cb), lambda bi, j: (0, j))
    vec2 = pl.BlockSpec((2, 1, cb), lambda bi, j: (0, 0, j))
    wspec = pl.BlockSpec((2, nrb, RG_BLOCK, RG_BLOCK), lambda bi, j: (0, j, 0, 0))
    in_specs = [pl.BlockSpec((t, cb), lambda bi, j: (rb0 + bi, j)),
                pl.BlockSpec((t, cb), lambda bi, j: (rb0 + bi, ncb + j)),
                vec(4), vec(1), wspec, wspec, vec2, vec2, vec2]
    args = [proj, proj, conv_w, conv_b.reshape(1, d), w_a, w_x,
            b_a.reshape(2, 1, d), b_x.reshape(2, 1, d), softplus_neg_lam.reshape(2, 1, d)]
    if s0 is not None:
        in_specs.append(pl.BlockSpec((None, 2, 1, cb), lambda bi, j: (bi, 0, 0, j)))
        args.append(s0.reshape(b, 2, 1, d))
    in_specs.append(pl.BlockSpec(memory_space=pl.ANY))
    args.append(y_mix)
    alias_idx = len(args) - 1
    out_specs = [pl.BlockSpec((t, cb), lambda bi, j: (rb0 + bi, j))]
    out_shape = [jax.ShapeDtypeStruct(y_mix.shape, y_mix.dtype)]
    if emit_state:
        out_specs.append(pl.BlockSpec((None, 2, 1, cb), lambda bi, j: (bi, 0, 0, j)))
        out_shape.append(jax.ShapeDtypeStruct((b, 2, 1, d), F32))
    seq = pltpu.VMEM((t, cb), F32)
    outs = pl.pallas_call(
        functools.partial(_odd_kernel, t=t, has_state=s0 is not None, emit_state=emit_state),
        grid=(b, ncb),
        in_specs=in_specs,
        out_specs=out_specs,
        out_shape=out_shape,
        scratch_shapes=[pltpu.VMEM((t + 2 * SUBLANES, cb), F32), seq, seq, seq, seq, seq, seq],
        input_output_aliases={alias_idx: 0},
        compiler_params=_cp(2),
    )(*args)
    return outs[0], (outs[1].reshape(b, 2, d) if emit_state else None)


def _route(logits):
    lane = lax.broadcasted_iota(jnp.int32, logits.shape, 1)
    neg = jnp.float32(-jnp.inf)
    big = jnp.int32(LANES)

    def arg_max(vals):
        m = jnp.max(vals, axis=-1, keepdims=True)
        return m, jnp.min(jnp.where(vals == m, lane, big), axis=-1, keepdims=True)

    g_logits = jnp.where(lane < N_GROUPS, logits, neg)
    g_max, g_sel = arg_max(g_logits)
    p_grp = 1.0 / jnp.sum(jnp.exp(g_logits - g_max), axis=-1, keepdims=True)
    lo = N_GROUPS + EXPERTS_PER_GROUP * g_sel
    e_logits = jnp.where((lane >= lo) & (lane < lo + EXPERTS_PER_GROUP), logits, neg)
    v1, i1 = arg_max(e_logits)
    v2, i2 = arg_max(jnp.where(lane == i1, neg, e_logits))
    e2 = jnp.exp(v2 - v1)
    w1 = p_grp / (1.0 + e2)
    w2 = p_grp * e2 / (1.0 + e2)
    ids = jnp.where(lane == 0, i1 - N_GROUPS, jnp.where(lane == 1, i2 - N_GROUPS, 0))
    wts = jnp.where(lane == 0, w1, jnp.where(lane == 1, w2, 0.0))
    return ids, wts


def _outproj_kernel(y_ref, w_ref, x_ref, g1_ref, gn_ref, sh_ref, sc_ref, wr_ref, br_ref,
                    xo_ref, h_ref, ids_ref, wts_ref):
    x = x_ref[...] + g1_ref[...] * _dot(y_ref[...], w_ref[...])
    xo_ref[...] = x
    h = _norm_mod(x, gn_ref[...], sc_ref[...], sh_ref[...])
    h_ref[...] = h
    logits = _dot(h, wr_ref[...], precision=HIGHEST) + br_ref[...]
    ids, wts = _route(logits)
    ids_ref[...] = ids
    wts_ref[...] = wts


def _outproj_route(y_mix, w_out, x, mod, g_norm2, w_route, b_route, rows, layer):
    n, d = x.shape
    tm = rows.tile(256)
    row_tile = pl.BlockSpec((tm, d), lambda i: (i, 0))
    slab = pl.BlockSpec((tm, LANES), lambda i: (i, 0))
    return pl.pallas_call(
        _outproj_kernel,
        grid=(n // tm,),
        in_specs=[row_tile,
                  pl.BlockSpec((d, d), lambda i: (0, 0)),
                  row_tile,
                  _mod_spec(rows, layer, 2, tm),
                  pl.BlockSpec((None, 1, d), lambda i: (layer, 0, 0)),
                  _mod_spec(rows, layer, 3, tm),
                  _mod_spec(rows, layer, 4, tm),
                  pl.BlockSpec((d, LANES), lambda i: (0, 0)),
                  pl.BlockSpec((1, LANES), lambda i: (0, 0))],
        out_specs=[row_tile, row_tile, slab, slab],
        out_shape=[jax.ShapeDtypeStruct((n, d), F32), jax.ShapeDtypeStruct((n, d), F32),
                   jax.ShapeDtypeStruct((n, LANES), jnp.int32), jax.ShapeDtypeStruct((n, LANES), F32)],
        compiler_params=_cp(1),
    )(y_mix, w_out, x, mod, g_norm2.reshape(-1, 1, d), mod, mod, w_route, b_route)


def _gather_rows(idx_ref, n_rows, src_hbm, dst, sem, col0=0):
    def row_copy(r, tok):
        return pltpu.make_async_copy(src_hbm.at[pl.ds(tok, 1)], dst.at[pl.ds(r, 1)], sem)

    def start(r, carry):
        row_copy(r, idx_ref[0, 0, col0 + r]).start()
        return carry

    def wait(r, carry):
        row_copy(r, 0).wait()
        return carry

    lax.fori_loop(0, n_rows, start, 0)
    lax.fori_loop(0, n_rows, wait, 0)


def _moe_kernel(tile_e_ref, src_ref, wslot_ref, h_hbm, w1_ref, w3_ref, w2_ref, o_ref, xbuf, sem):
    del tile_e_ref
    tm = xbuf.shape[0]
    _gather_rows(src_ref, tm, h_hbm, xbuf, sem)
    x = xbuf[...].astype(BF16)
    hid = _silu(_dot(x, w1_ref[...])) * _dot(x, w3_ref[...])
    o_ref[...] = _dot(hid.astype(BF16), w2_ref[...]) * wslot_ref[...]


def _moe_experts(h2, tile_expert, src_rows, slot_w, w1, w3, w2):
    n, d = h2.shape
    n_tiles = tile_expert.shape[0]
    tm = MOE_TM
    r = n_tiles * tm
    grid_spec = pltpu.PrefetchScalarGridSpec(
        num_scalar_prefetch=1,
        grid=(n_tiles,),
        in_specs=[pl.BlockSpec((1, 1, tm), lambda i, te: (i, 0, 0), memory_space=pltpu.SMEM),
                  pl.BlockSpec((tm, 1), lambda i, te: (i, 0)),
                  pl.BlockSpec(memory_space=pl.ANY),
                  pl.BlockSpec((None, d, D_EXPERT), lambda i, te: (te[i], 0, 0)),
                  pl.BlockSpec((None, d, D_EXPERT), lambda i, te: (te[i], 0, 0)),
                  pl.BlockSpec((None, D_EXPERT, d), lambda i, te: (te[i], 0, 0))],
        out_specs=pl.BlockSpec((tm, d), lambda i, te: (i, 0)),
        scratch_shapes=[pltpu.VMEM((tm, d), F32), pltpu.SemaphoreType.DMA(())],
    )
    return pl.pallas_call(
        _moe_kernel,
        grid_spec=grid_spec,
        out_shape=jax.ShapeDtypeStruct((r, d), F32),
        compiler_params=_cp(1),
    )(tile_expert, src_rows.reshape(n_tiles, 1, tm), slot_w.reshape(r, 1), h2, w1, w3, w2)


def _plan_routing(ids, wts):
    n = ids.shape[0]
    tm = MOE_TM
    n_tiles = (2 * n) // tm + N_EXPERTS
    r = n_tiles * tm
    e_flat = ids[:, :2].reshape(-1)
    w_flat = wts[:, :2].reshape(-1)
    onehot = (e_flat[:, None] == jnp.arange(N_EXPERTS, dtype=jnp.int32)[None, :]).astype(jnp.int32)
    csum = jnp.cumsum(onehot, axis=0)
    rank = jnp.sum(onehot * csum, axis=1) - 1
    counts = csum[-1]
    padded = ((counts + tm - 1) // tm) * tm
    ends = jnp.cumsum(padded)
    offs = ends - padded
    dest = jnp.sum(onehot * offs[None, :], axis=1) + rank
    tile_start = jnp.arange(n_tiles, dtype=jnp.int32) * tm
    tile_expert = jnp.minimum(jnp.sum((tile_start[:, None] >= ends[None, :]).astype(jnp.int32), axis=1),
                              N_EXPERTS - 1).astype(jnp.int32)
    tok = jnp.arange(2 * n, dtype=jnp.int32) // 2
    src_rows = jnp.zeros((r,), jnp.int32).at[dest].set(tok)
    slot_w = jnp.zeros((r,), F32).at[dest].set(w_flat)
    return tile_expert, src_rows, slot_w, dest.reshape(n, 2)


def _combine_kernel(dest_ref, x_ref, g2_ref, gn_ref, sh_ref, sc_ref, ys_hbm, xo_ref, h_ref, buf, sem, *, final):
    tm = x_ref.shape[0]
    _gather_rows(dest_ref, 2 * tm, ys_hbm, buf, sem)
    x = x_ref[...] + g2_ref[...] * (buf[pl.ds(0, tm), :] + buf[pl.ds(tm, tm), :])
    xo_ref[...] = x
    if final:
        ms = jnp.mean(x * x, axis=-1, keepdims=True)
        h_ref[...] = x * lax.rsqrt(ms + EPS) * gn_ref[...]
    else:
        h_ref[...] = _norm_mod(x, gn_ref[...], sc_ref[...], sh_ref[...]).astype(BF16)


def _combine(ys, dest, x, mod, g_next, rows, layer, final):
    n, d = x.shape
    tm = rows.tile(256)
    n_tiles = n // tm
    dest3 = dest.reshape(n_tiles, tm, 2).transpose(0, 2, 1).reshape(n_tiles, 1, 2 * tm)
    row_tile = pl.BlockSpec((tm, d), lambda i: (i, 0))
    nxt = layer if final else layer + 1
    gain_spec = (pl.BlockSpec((1, d), lambda i: (0, 0)) if final
                 else pl.BlockSpec((None, 1, d), lambda i: (nxt, 0, 0)))
    gain = g_next.reshape(1, d) if final else g_next.reshape(-1, 1, d)
    return pl.pallas_call(
        functools.partial(_combine_kernel, final=final),
        grid=(n_tiles,),
        in_specs=[pl.BlockSpec((1, 1, 2 * tm), lambda i: (i, 0, 0), memory_space=pltpu.SMEM),
                  row_tile,
                  _mod_spec(rows, layer, 5, tm),
                  gain_spec,
                  _mod_spec(rows, nxt, 0, tm),
                  _mod_spec(rows, nxt, 1, tm),
                  pl.BlockSpec(memory_space=pl.ANY)],
        out_specs=[row_tile, row_tile],
        out_shape=[jax.ShapeDtypeStruct((n, d), F32), jax.ShapeDtypeStruct((n, d), F32 if final else BF16)],
        scratch_shapes=[pltpu.VMEM((2 * tm, d), F32), pltpu.SemaphoreType.DMA(())],
        compiler_params=_cp(1),
    )(dest3, x, mod, gain, mod, mod, ys)


def kernel(x_prompt, x_sample, state_ret, state_hgrn, state_rglru, c, c_ctx, w_ada, b_ada, g_norm1, g_norm2, w_even_in, w_even_out, ret_decay, hg_lb_logits, g_ret_head, g_hg_head, w_odd_in, conv_w, conv_b, w_a, b_a, w_x, b_x, rg_lambda, w_odd_out, w_group, b_group, w_router, b_router, w1, w3, w2, g_final):
    bp, tp, d = x_prompt.shape
    bs, ts, _ = x_sample.shape
    depth = w_ada.shape[0]
    rows = _Rows(bp, tp, bs, ts)
    assert bs + 1 <= COND_ROWS

    lb_sm = jax.nn.softmax(hg_lb_logits.astype(F32), axis=0)
    lb_all = jnp.cumsum(lb_sm, axis=0) - lb_sm[0:1]
    log_gamma = -jnp.exp(ret_decay.astype(F32))
    softplus_neg_lam = jax.nn.softplus(-rg_lambda.astype(F32))
    t_idx = jnp.arange(ts)
    freqs = ROPE_BASE ** (-jnp.arange(HEAD_DIM // 4, dtype=F32) / (HEAD_DIM // 4))
    ang = jnp.concatenate([(t_idx // GRID_W).astype(F32)[:, None] * freqs,
                           (t_idx % GRID_W).astype(F32)[:, None] * freqs], axis=-1)
    rope_tabs = (jnp.concatenate([jnp.cos(ang), jnp.cos(ang)], axis=-1),
                 jnp.concatenate([-jnp.sin(ang), jnp.sin(ang)], axis=-1))
    w_route = jnp.concatenate(
        [w_group, w_router, jnp.zeros((depth, d, LANES - N_GROUPS - N_EXPERTS), F32)], axis=-1)
    b_route = jnp.concatenate(
        [b_group, b_router, jnp.zeros((depth, LANES - N_GROUPS - N_EXPERTS), F32)], axis=-1)
    bf = lambda w: w.astype(BF16)

    cond = jnp.zeros((COND_ROWS, d), F32).at[:bs].set(c).at[bs].set(c_ctx)
    mod = _adaln_all(cond, w_ada, b_ada)

    x = jnp.concatenate([x_prompt.reshape(bp * tp, d), x_sample.reshape(bs * ts, d)], axis=0)
    h = _norm0(x, g_norm1, mod, rows, 0)
    new_ret, new_hg, new_rg = [], [], []
    for l in range(depth):
        y_mix = jnp.zeros((rows.n, d), BF16)
        if l % 2 == 0:
            e = l // 2
            proj = _in_proj(h, bf(w_even_in[e]))
            y_mix, sr = _retention(proj, y_mix, log_gamma[e], g_ret_head[e], 0, bp, tp, None, None, True)
            y_mix, _ = _retention(proj, y_mix, log_gamma[e], g_ret_head[e], rows.n_ctx, bs, ts, rope_tabs,
                                  state_ret[:, e], False)
            y_mix, sh = _hgrn2(proj, y_mix, lb_all[e], g_hg_head[e], 0, bp, tp, None, True)
            y_mix, _ = _hgrn2(proj, y_mix, lb_all[e], g_hg_head[e], rows.n_ctx, bs, ts, state_hgrn[:, e], False)
            new_ret.append(sr)
            new_hg.append(sh)
            w_out = bf(w_even_out[e])
        else:
            o = l // 2
            proj = _in_proj(h, bf(w_odd_in[o]))
            wa, wx = bf(w_a[o]), bf(w_x[o])
            y_mix, sg = _rglru(proj, y_mix, conv_w[o], conv_b[o], wa, wx, b_a[o], b_x[o], softplus_neg_lam[o],
                               0, bp, tp, None, True)
            y_mix, _ = _rglru(proj, y_mix, conv_w[o], conv_b[o], wa, wx, b_a[o], b_x[o], softplus_neg_lam[o],
                              rows.n_ctx, bs, ts, state_rglru[:, o], False)
            new_rg.append(sg)
            w_out = bf(w_odd_out[o])
        x, h2, ids, wts = _outproj_route(y_mix, w_out, x, mod, g_norm2, w_route[l], b_route[l:l + 1], rows, l)
        tile_expert, src_rows, slot_w, dest = _plan_routing(ids, wts)
        ys = _moe_experts(h2, tile_expert, src_rows, slot_w, bf(w1[l]), bf(w3[l]), bf(w2[l]))
        final = l == depth - 1
        x, h = _combine(ys, dest, x, mod, g_final if final else g_norm1, rows, l, final)

    y_prompt = h[:rows.n_ctx].reshape(bp, tp, d)
    y_sample = h[rows.n_ctx:].reshape(bs, ts, d)
    return (y_prompt, y_sample, jnp.stack(new_ret, axis=1), jnp.stack(new_hg, axis=1),
            jnp.stack(new_rg, axis=1))
```

```python
import functools

import numpy as np
import jax
import jax.numpy as jnp
from jax import lax
from jax.experimental import pallas as pl
from jax.experimental.pallas import tpu as pltpu

F32 = jnp.float32
BF16 = jnp.bfloat16
HIGHEST = lax.Precision.HIGHEST

D_MODEL = 2048
GRID_W = 64
HEAD_DIM = 128
N_HEADS = 8
MIX_HALF = N_HEADS * HEAD_DIM
EVEN_IN = 9 * MIX_HALF
ROPE_BASE = 10000.0
RG_BLOCK = 128
RG_C = 8.0
N_GROUPS = 4
EXPERTS_PER_GROUP = 4
N_EXPERTS = N_GROUPS * EXPERTS_PER_GROUP
D_EXPERT = 512
EPS = 1e-6
F_MIN = 1e-20
COND_ROWS = 16
LANES = 128
SUBLANES = 8

RET_CHUNK = 128
HG_CHUNK = 128
HG_UNROLL = 2
ODD_CB = 512
ROUTE_COLS = N_GROUPS + N_EXPERTS
MOE_TM = 256
VMEM_LIMIT = 52 * 1024 * 1024


def _cp(n_axes, vmem=VMEM_LIMIT):
    return pltpu.CompilerParams(dimension_semantics=("arbitrary",) * n_axes, vmem_limit_bytes=vmem)


def _silu(x):
    return x * jax.nn.sigmoid(x)


def _sigmoid_tanh(x):
    return 0.5 + 0.5 * jnp.tanh(0.5 * x)


def _dot(a, b, **kw):
    return jnp.dot(a, b, preferred_element_type=F32, **kw)


def _dot_nt(a, b):
    return lax.dot_general(a, b, (((1,), (1,)), ((), ())), preferred_element_type=F32)


def _dot_tn(a, b):
    return lax.dot_general(a, b, (((0,), (0,)), ((), ())), preferred_element_type=F32)


def _norm_mod(x, gain, scale, shift):
    ms = jnp.mean(x * x, axis=-1, keepdims=True)
    return x * lax.rsqrt(ms + EPS) * gain * (1.0 + scale) + shift


def _ada_kernel(cond_ref, w_ref, b_ref, o_ref):
    o_ref[...] = _dot(_silu(cond_ref[...]), w_ref[...], precision=HIGHEST) + b_ref[...]


def _adaln_all(cond, w_ada, b_ada):
    depth, d, n6 = w_ada.shape
    tn = 1024
    out = pl.pallas_call(
        _ada_kernel,
        grid=(depth, n6 // tn),
        in_specs=[pl.BlockSpec((COND_ROWS, d), lambda l, j: (0, 0)),
                  pl.BlockSpec((None, d, tn), lambda l, j: (l, 0, j)),
                  pl.BlockSpec((None, 1, tn), lambda l, j: (l, 0, j))],
        out_specs=pl.BlockSpec((None, COND_ROWS, tn), lambda l, j: (l, 0, j)),
        out_shape=jax.ShapeDtypeStruct((depth, COND_ROWS, n6), F32),
        compiler_params=_cp(2),
    )(cond, w_ada, b_ada.reshape(depth, 1, n6))
    return out.reshape(depth, COND_ROWS, 6, 1, d)


class _Rows:
    def __init__(self, bp, tp, bs, ts):
        self.bp, self.tp, self.bs, self.ts = bp, tp, bs, ts
        self.n_ctx = bp * tp
        self.n = self.n_ctx + bs * ts
        self.ctx_row = bs

    def cond_row(self, i, tm):
        n_ctx_tiles = self.n_ctx // tm
        per_seq = self.ts // tm
        return jnp.where(i < n_ctx_tiles, self.ctx_row, (i - n_ctx_tiles) // per_seq)

    def tile(self, cap):
        tm = min(cap, self.ts)
        assert self.ts % tm == 0 and self.n_ctx % tm == 0
        return tm


def _mod_spec(rows, layer, which, tm):
    d = D_MODEL
    return pl.BlockSpec((None, None, None, 1, d),
                        lambda i, *_: (layer, rows.cond_row(i, tm), which, 0, 0))


def _norm0_kernel(x_ref, g_ref, sh_ref, sc_ref, h_ref):
    h_ref[...] = _norm_mod(x_ref[...], g_ref[...], sc_ref[...], sh_ref[...]).astype(BF16)


def _norm0(x, g_norm, mod, rows, layer):
    n, d = x.shape
    tm = rows.tile(512)
    return pl.pallas_call(
        _norm0_kernel,
        grid=(n // tm,),
        in_specs=[pl.BlockSpec((tm, d), lambda i: (i, 0)),
                  pl.BlockSpec((None, 1, d), lambda i: (layer, 0, 0)),
                  _mod_spec(rows, layer, 0, tm),
                  _mod_spec(rows, layer, 1, tm)],
        out_specs=pl.BlockSpec((tm, d), lambda i: (i, 0)),
        out_shape=jax.ShapeDtypeStruct((n, d), BF16),
        compiler_params=_cp(1),
    )(x, g_norm.reshape(-1, 1, d), mod, mod)


def _matmul_kernel(h_ref, w_ref, o_ref):
    o_ref[...] = _dot(h_ref[...], w_ref[...])


def _in_proj(h, w):
    n, k = h.shape
    _, nout = w.shape
    tm = 1024 if n % 1024 == 0 else 128
    tn = 512
    return pl.pallas_call(
        _matmul_kernel,
        grid=(n // tm, nout // tn),
        in_specs=[pl.BlockSpec((tm, k), lambda i, j: (i, 0)),
                  pl.BlockSpec((k, tn), lambda i, j: (0, j))],
        out_specs=pl.BlockSpec((tm, tn), lambda i, j: (i, j)),
        out_shape=jax.ShapeDtypeStruct((n, nout), F32),
        compiler_params=_cp(2),
    )(h, w)


def _ret_kernel(*refs, t, rope, has_state, emit_state):
    refs = list(refs)
    lg_ref = refs.pop(0)
    q_ref, k_ref, v_ref, g_ref, gain_ref = [refs.pop(0) for _ in range(5)]
    if rope:
        cos_ref, sin_ref = refs.pop(0), refs.pop(0)
    if has_state:
        s0_ref = refs.pop(0)
    refs.pop(0)
    o_ref = refs.pop(0)
    if emit_state:
        st_ref = refs.pop(0)
    sb_scr, qs, ks = refs

    c = min(RET_CHUNK, t)
    n_chunks = t // c
    head = pl.program_id(1)
    lgf = lg_ref[0, head]
    lgb = lg_ref[1, head]

    q = q_ref[...]
    k = k_ref[...]
    if rope:
        cs, sn = cos_ref[...], sin_ref[...]
        q = q * cs + pltpu.roll(q, HEAD_DIM // 2, 1) * sn
        k = k * cs + pltpu.roll(k, HEAD_DIM // 2, 1) * sn
    qs[...] = q
    ks[...] = k * (HEAD_DIM ** -0.5)

    pos_c = lax.broadcasted_iota(jnp.int32, (c, 1), 0).astype(F32)
    pos_r = lax.broadcasted_iota(jnp.int32, (1, c), 1).astype(F32)
    rel = pos_c - pos_r
    decay = (jnp.where(rel >= 0, jnp.exp(lgf * jnp.maximum(rel, 0.0)), 0.0)
             + jnp.where(rel <= 0, jnp.exp(lgb * jnp.maximum(-rel, 0.0)), 0.0))
    qdf = jnp.exp(lgf * (pos_c + 1.0))
    kdf = jnp.exp(lgf * (c - 1.0 - pos_c))
    qdb = jnp.exp(lgb * (c - pos_c))
    kdb = jnp.exp(lgb * pos_c)
    full = jnp.full((1, HEAD_DIM), float(c), F32)
    cdf = jnp.exp(lgf * full)
    cdb = jnp.exp(lgb * full)

    def rows_of(i):
        return pl.ds(pl.multiple_of(i * c, c), c)

    def bwd_body(ii, s):
        i = n_chunks - 1 - ii
        sb_scr[i] = s
        sl = rows_of(i)
        kv = _dot_tn((ks[sl, :] * kdb).astype(BF16), v_ref[sl, :].astype(BF16))
        return s * cdb + kv

    s0_b = s0_ref[1] if has_state else jnp.zeros((HEAD_DIM, HEAD_DIM), F32)
    s_b = lax.fori_loop(0, n_chunks, bwd_body, s0_b, unroll=True)

    gain = gain_ref[...]

    def fwd_body(i, s):
        sl = rows_of(i)
        qc, kc = qs[sl, :], ks[sl, :]
        vb = v_ref[sl, :].astype(BF16)
        scores = _dot_nt(qc.astype(BF16), kc.astype(BF16)) * decay
        o = (_dot(scores.astype(BF16), vb)
             + _dot((qc * qdf).astype(BF16), s.astype(BF16))
             + _dot((qc * qdb).astype(BF16), sb_scr[i].astype(BF16)))
        ms = jnp.mean(o * o, axis=-1, keepdims=True)
        o = o * lax.rsqrt(ms + EPS) * gain
        o_ref[sl, :] = (o * _silu(g_ref[sl, :])).astype(BF16)
        return s * cdf + _dot_tn((kc * kdf).astype(BF16), vb)

    s0_f = s0_ref[0] if has_state else jnp.zeros((HEAD_DIM, HEAD_DIM), F32)
    s_f = lax.fori_loop(0, n_chunks, fwd_body, s0_f, unroll=True)
    if emit_state:
        st_ref[0] = s_f
        st_ref[1] = s_b


def _retention(proj, y_mix, log_gamma, g_head, row0, b, t, rope_tabs, s0, emit_state):
    rb0 = row0 // t
    blk = lambda col0: pl.BlockSpec((t, HEAD_DIM), lambda bi, h: (rb0 + bi, col0 + h))
    in_specs = [pl.BlockSpec(memory_space=pltpu.SMEM),
                blk(0), blk(N_HEADS), blk(2 * N_HEADS), blk(3 * N_HEADS),
                pl.BlockSpec((None, 1, HEAD_DIM), lambda bi, h: (h, 0, 0))]
    args = [log_gamma, proj, proj, proj, proj, g_head.reshape(N_HEADS, 1, HEAD_DIM)]
    if rope_tabs is not None:
        in_specs += [pl.BlockSpec((t, HEAD_DIM), lambda bi, h: (0, 0))] * 2
        args += list(rope_tabs)
    if s0 is not None:
        in_specs.append(pl.BlockSpec((None, 2, None, HEAD_DIM, HEAD_DIM), lambda bi, h: (bi, 0, h, 0, 0)))
        args.append(s0)
    in_specs.append(pl.BlockSpec(memory_space=pl.ANY))
    args.append(y_mix)
    alias_idx = len(args) - 1
    out_specs = [pl.BlockSpec((t, HEAD_DIM), lambda bi, h: (rb0 + bi, h))]
    out_shape = [jax.ShapeDtypeStruct(y_mix.shape, y_mix.dtype)]
    if emit_state:
        out_specs.append(pl.BlockSpec((None, 2, None, HEAD_DIM, HEAD_DIM), lambda bi, h: (bi, 0, h, 0, 0)))
        out_shape.append(jax.ShapeDtypeStruct((b, 2, N_HEADS, HEAD_DIM, HEAD_DIM), F32))
    n_chunks = t // min(RET_CHUNK, t)
    outs = pl.pallas_call(
        functools.partial(_ret_kernel, t=t, rope=rope_tabs is not None, has_state=s0 is not None,
                          emit_state=emit_state),
        grid=(b, N_HEADS),
        in_specs=in_specs,
        out_specs=out_specs,
        out_shape=out_shape,
        scratch_shapes=[pltpu.VMEM((n_chunks, HEAD_DIM, HEAD_DIM), F32),
                        pltpu.VMEM((t, HEAD_DIM), F32),
                        pltpu.VMEM((t, HEAD_DIM), F32)],
        input_output_aliases={alias_idx: 0},
        compiler_params=_cp(2),
    )(*args)
    return outs[0], (outs[1] if emit_state else None)


def _hg_plan(c):
    t = np.arange(c)
    lower = (t[:, None] >= t[None, :]).astype(np.float32)
    stacks, masks = [], []
    for fwd in (True, False):
        tri = lower if fwd else lower.T
        end = c - 1 if fwd else 0
        blocks = [tri, tri[end:end + 1, :] - tri]
        level_masks = []
        w = 1
        while w < c:
            blk = t // (2 * w)
            late = (t % (2 * w)) >= w
            query = late if fwd else ~late
            ref = blk * 2 * w + (w - 1 if fwd else w)
            blocks.append(np.where(query, 1.0, -1.0)[:, None] * (tri - tri[ref, :]))
            level_masks.append(((blk[:, None] == blk[None, :]) & query[:, None] & ~query[None, :])
                               .astype(np.float32))
            w *= 2
        m = np.concatenate(blocks, axis=0)
        stacks.append(np.concatenate([m, m, m], axis=1))
        masks.append(np.stack(level_masks))
    return np.stack(stacks), np.stack(masks)


def _split3(x):
    hi = x.astype(BF16)
    r1 = x - hi.astype(F32)
    mid = r1.astype(BF16)
    lo = (r1 - mid.astype(F32)).astype(BF16)
    return hi, mid, lo


def _hg_kernel(*refs, t, has_state, emit_state):
    refs = list(refs)
    (q_ref, ff_ref, fb_ref, v_ref, gate_ref, lb_ref, gain_ref, m_ref, mask_ref) = [refs.pop(0) for _ in range(9)]
    if has_state:
        s0_ref = refs.pop(0)
    refs.pop(0)
    o_ref = refs.pop(0)
    if emit_state:
        st_ref = refs.pop(0)
    qa, k_fw, l_fw, k_bw, l_bw, o_fw, o_bw = refs

    c = min(HG_CHUNK, t)
    n_chunks = t // c
    n_levels = mask_ref.shape[1]
    lbh = lb_ref[...]
    qa[...] = _silu(q_ref[...])

    def forget(z_ref, k_out, l_out):
        sg = jax.nn.sigmoid(z_ref[...])
        l_out[...] = jnp.log(jnp.maximum(lbh + (1.0 - lbh) * sg, F_MIN))
        k_out[...] = (1.0 - lbh) * (1.0 - sg)

    forget(ff_ref, k_fw, l_fw)
    forget(fb_ref, k_bw, l_bw)

    row = lax.broadcasted_iota(jnp.int32, (c, c), 0)
    col = lax.broadcasted_iota(jnp.int32, (c, c), 1)

    def rows_of(i):
        return pl.ds(pl.multiple_of(i * c, c), c)

    def direction(d, q, k, log_f, vb, s, p):
        e = jnp.exp(_dot(m_ref[d], jnp.concatenate(_split3(log_f), axis=0)))
        for j in range(n_levels):
            ej = e[(2 + j) * c:(3 + j) * c]
            p = p + _dot_nt((q * ej).astype(BF16), (k * ej).astype(BF16)) * mask_ref[d, j]
        o = _dot(p.astype(BF16), vb) + _dot_nt((q * e[0:c]).astype(BF16), s.astype(BF16))
        end = c - 1 if d == 0 else 0
        s = s * e[end:end + 1] + _dot_tn(vb, (k * e[c:2 * c]).astype(BF16))
        return o, s

    def body(i, carry):
        s_f, s_b = carry
        sl = rows_of(i)
        q = qa[sl, :]
        kf = k_fw[sl, :]
        own = jnp.where(row == col, _dot_nt(q.astype(BF16), (kf + k_bw[sl, :]).astype(BF16)), 0.0)
        o, s_f = direction(0, q, kf, l_fw[sl, :], v_ref[sl, :].astype(BF16), s_f, own)
        o_fw[sl, :] = o
        sl = rows_of(n_chunks - 1 - i)
        o, s_b = direction(1, qa[sl, :], k_bw[sl, :], l_bw[sl, :], v_ref[sl, :].astype(BF16), s_b,
                           jnp.zeros((c, c), F32))
        o_bw[sl, :] = o
        return s_f, s_b

    zero = jnp.zeros((HEAD_DIM, HEAD_DIM), F32)
    s_f, s_b = lax.fori_loop(0, n_chunks, body,
                             (s0_ref[0].T, s0_ref[1].T) if has_state else (zero, zero),
                             unroll=min(HG_UNROLL, n_chunks))
    o = o_fw[...] + o_bw[...]
    ms = jnp.mean(o * o, axis=-1, keepdims=True)
    o = o * lax.rsqrt(ms + EPS) * gain_ref[...]
    o_ref[...] = (o * _silu(gate_ref[...])).astype(BF16)
    if emit_state:
        st_ref[0] = s_f.T
        st_ref[1] = s_b.T


def _hgrn2(proj, y_mix, lb, g_head, row0, b, t, s0, emit_state):
    rb0 = row0 // t
    c = min(HG_CHUNK, t)
    m_np, mask_np = _hg_plan(c)
    m_stack = jnp.asarray(m_np, BF16)
    lvl_mask = jnp.asarray(mask_np, F32)
    blk = lambda col0: pl.BlockSpec((t, HEAD_DIM), lambda bi, h: (rb0 + bi, col0 + h))
    per_head = pl.BlockSpec((None, 1, HEAD_DIM), lambda bi, h: (h, 0, 0))
    in_specs = [blk(4 * N_HEADS), blk(5 * N_HEADS), blk(6 * N_HEADS), blk(7 * N_HEADS), blk(8 * N_HEADS),
                per_head, per_head,
                pl.BlockSpec(m_stack.shape, lambda bi, h: (0, 0, 0)),
                pl.BlockSpec(lvl_mask.shape, lambda bi, h: (0, 0, 0, 0))]
    args = [proj] * 5 + [lb.reshape(N_HEADS, 1, HEAD_DIM), g_head.reshape(N_HEADS, 1, HEAD_DIM),
                         m_stack, lvl_mask]
    if s0 is not None:
        in_specs.append(pl.BlockSpec((None, 2, None, HEAD_DIM, HEAD_DIM), lambda bi, h: (bi, 0, h, 0, 0)))
        args.append(s0)
    in_specs.append(pl.BlockSpec(memory_space=pl.ANY))
    args.append(y_mix)
    alias_idx = len(args) - 1
    out_specs = [pl.BlockSpec((t, HEAD_DIM), lambda bi, h: (rb0 + bi, N_HEADS + h))]
    out_shape = [jax.ShapeDtypeStruct(y_mix.shape, y_mix.dtype)]
    if emit_state:
        out_specs.append(pl.BlockSpec((None, 2, None, HEAD_DIM, HEAD_DIM), lambda bi, h: (bi, 0, h, 0, 0)))
        out_shape.append(jax.ShapeDtypeStruct((b, 2, N_HEADS, HEAD_DIM, HEAD_DIM), F32))
    seq = pltpu.VMEM((t, HEAD_DIM), F32)
    outs = pl.pallas_call(
        functools.partial(_hg_kernel, t=t, has_state=s0 is not None, emit_state=emit_state),
        grid=(b, N_HEADS),
        in_specs=in_specs,
        out_specs=out_specs,
        out_shape=out_shape,
        scratch_shapes=[seq] * 7,
        input_output_aliases={alias_idx: 0},
        compiler_params=_cp(2),
    )(*args)
    return outs[0], (outs[1] if emit_state else None)


def _block_scan(a, b, reverse):
    row = lax.broadcasted_iota(jnp.int32, a.shape, 0)
    k = 1
    while k < SUBLANES:
        shift = (SUBLANES - k) if reverse else k
        valid = (row < SUBLANES - k) if reverse else (row >= k)
        a_prev = pltpu.roll(a, shift, 0)
        b_prev = pltpu.roll(b, shift, 0)
        b = jnp.where(valid, a * b_prev + b, b)
        a = jnp.where(valid, a * a_prev, a)
        k *= 2
    return a, b


def _odd_kernel(*refs, t, has_state, emit_state):
    refs = list(refs)
    (gi_ref, xb_ref, cw_ref, cb_ref, wa_ref, wx_ref, ba_ref, bx_ref, sp_ref) = [refs.pop(0) for _ in range(9)]
    if has_state:
        s0_ref = refs.pop(0)
    refs.pop(0)
    o_ref = refs.pop(0)
    if emit_state:
        st_ref = refs.pop(0)
    xpad, xc, a_f, b_f, a_b, b_b, h_f = refs

    cb = xb_ref.shape[1]
    pad = SUBLANES
    zeros = jnp.zeros((pad, cb), F32)
    xpad[pl.ds(0, pad), :] = zeros
    xpad[pl.ds(pad + t, pad), :] = zeros
    xpad[pl.ds(pad, t), :] = xb_ref[...]
    for n in range(cb // RG_BLOCK):
        cols = pl.ds(n * RG_BLOCK, RG_BLOCK)
        xc[:, cols] = (cb_ref[:, cols]
                       + xpad[pl.ds(pad - 1, t), cols] * cw_ref[0:1, cols]
                       + xpad[pl.ds(pad, t), cols] * cw_ref[1:2, cols]
                       + xpad[pl.ds(pad + 1, t), cols] * cw_ref[2:3, cols]
                       + xpad[pl.ds(pad + 2, t), cols] * cw_ref[3:4, cols])

    rc = min(256, t)

    def gate_body(i, carry):
        sl = pl.ds(pl.multiple_of(i * rc, rc), rc)
        for n in range(cb // RG_BLOCK):
            cols = pl.ds(n * RG_BLOCK, RG_BLOCK)
            x_blk = xc[sl, cols]
            x_bf = x_blk.astype(BF16)
            for d, (a_out, b_out) in enumerate(((a_f, b_f), (a_b, b_b))):
                r = _sigmoid_tanh(_dot(x_bf, wa_ref[d, n]) + ba_ref[d, :, cols])
                gate_i = _sigmoid_tanh(_dot(x_bf, wx_ref[d, n]) + bx_ref[d, :, cols])
                log_a = -RG_C * r * sp_ref[d, :, cols]
                a = jnp.exp(log_a)
                a_out[sl, cols] = a
                z = jnp.maximum(-jnp.tanh(log_a) * (1.0 + a * a), F_MIN)
                b_out[sl, cols] = (z * lax.rsqrt(z)) * (gate_i * x_blk)
        return carry

    lax.fori_loop(0, t // rc, gate_body, 0)

    n_blocks = t // SUBLANES

    def rows_of(j):
        return pl.ds(pl.multiple_of(j * SUBLANES, SUBLANES), SUBLANES)

    def fwd_body(j, h_prev):
        sl = rows_of(j)
        a, b = _block_scan(a_f[sl, :], b_f[sl, :], False)
        h = a * h_prev + b
        h_f[sl, :] = h
        return jnp.broadcast_to(h[SUBLANES - 1:SUBLANES, :], h.shape)

    h0_f = s0_ref[0] if has_state else jnp.zeros((1, cb), F32)
    last_f = lax.fori_loop(0, n_blocks, fwd_body, jnp.broadcast_to(h0_f, (SUBLANES, cb)))

    def bwd_body(jj, h_next):
        sl = rows_of(n_blocks - 1 - jj)
        a, b = _block_scan(a_b[sl, :], b_b[sl, :], True)
        h = a * h_next + b
        gate = jax.nn.gelu(gi_ref[sl, :], approximate=True)
        o_ref[sl, :] = ((h_f[sl, :] + h) * gate).astype(BF16)
        return jnp.broadcast_to(h[0:1, :], h.shape)

    h0_b = s0_ref[1] if has_state else jnp.zeros((1, cb), F32)
    first_b = lax.fori_loop(0, n_blocks, bwd_body, jnp.broadcast_to(h0_b, (SUBLANES, cb)))
    if emit_state:
        st_ref[0] = last_f[0:1, :]
        st_ref[1] = first_b[0:1, :]


def _rglru(proj, y_mix, conv_w, conv_b, w_a, w_x, b_a, b_x, softplus_neg_lam, row0, b, t, s0, emit_state):
    d = D_MODEL
    cb = ODD_CB
    ncb = d // cb
    nrb = cb // RG_BLOCK
    rb0 = row0 // t
    vec = lambda rows: pl.BlockSpec((rows, cb), lambda bi, j: (0, j))
    vec2 = pl.BlockSpec((2, 1, cb), lambda bi, j: (0, 0, j))
    wspec = pl.BlockSpec((2, nrb, RG_BLOCK, RG_BLOCK), lambda bi, j: (0, j, 0, 0))
    in_specs = [pl.BlockSpec((t, cb), lambda bi, j: (rb0 + bi, j)),
                pl.BlockSpec((t, cb), lambda bi, j: (rb0 + bi, ncb + j)),
                vec(4), vec(1), wspec, wspec, vec2, vec2, vec2]
    args = [proj, proj, conv_w, conv_b.reshape(1, d), w_a, w_x,
            b_a.reshape(2, 1, d), b_x.reshape(2, 1, d), softplus_neg_lam.reshape(2, 1, d)]
    if s0 is not None:
        in_specs.append(pl.BlockSpec((None, 2, 1, cb), lambda bi, j: (bi, 0, 0, j)))
        args.append(s0.reshape(b, 2, 1, d))
    in_specs.append(pl.BlockSpec(memory_space=pl.ANY))
    args.append(y_mix)
    alias_idx = len(args) - 1
    out_specs = [pl.BlockSpec((t, cb), lambda bi, j: (rb0 + bi, j))]
    out_shape = [jax.ShapeDtypeStruct(y_mix.shape, y_mix.dtype)]
    if emit_state:
        out_specs.append(pl.BlockSpec((None, 2, 1, cb), lambda bi, j: (bi, 0, 0, j)))
        out_shape.append(jax.ShapeDtypeStruct((b, 2, 1, d), F32))
    seq = pltpu.VMEM((t, cb), F32)
    outs = pl.pallas_call(
        functools.partial(_odd_kernel, t=t, has_state=s0 is not None, emit_state=emit_state),
        grid=(b, ncb),
        in_specs=in_specs,
        out_specs=out_specs,
        out_shape=out_shape,
        scratch_shapes=[pltpu.VMEM((t + 2 * SUBLANES, cb), F32), seq, seq, seq, seq, seq, seq],
        input_output_aliases={alias_idx: 0},
        compiler_params=_cp(2),
    )(*args)
    return outs[0], (outs[1].reshape(b, 2, d) if emit_state else None)


def _route(logits):
    lane = lax.broadcasted_iota(jnp.int32, logits.shape, 1)
    neg = jnp.float32(-jnp.inf)
    big = jnp.int32(LANES)

    def arg_max(vals):
        m = jnp.max(vals, axis=-1, keepdims=True)
        return m, jnp.min(jnp.where(vals == m, lane, big), axis=-1, keepdims=True)

    g_logits = jnp.where(lane < N_GROUPS, logits, neg)
    g_max, g_sel = arg_max(g_logits)
    p_grp = 1.0 / jnp.sum(jnp.exp(g_logits - g_max), axis=-1, keepdims=True)
    lo = N_GROUPS + EXPERTS_PER_GROUP * g_sel
    e_logits = jnp.where((lane >= lo) & (lane < lo + EXPERTS_PER_GROUP), logits, neg)
    v1, i1 = arg_max(e_logits)
    v2, i2 = arg_max(jnp.where(lane == i1, neg, e_logits))
    e2 = jnp.exp(v2 - v1)
    w1 = p_grp / (1.0 + e2)
    w2 = p_grp * e2 / (1.0 + e2)
    ids = jnp.where(lane == 0, i1 - N_GROUPS, jnp.where(lane == 1, i2 - N_GROUPS, 0))
    wts = jnp.where(lane == 0, w1, jnp.where(lane == 1, w2, 0.0))
    return ids, wts


def _outproj_kernel(y_ref, w_ref, x_ref, g1_ref, gn_ref, sh_ref, sc_ref, wr_ref, br_ref,
                    xo_ref, h_ref, ids_ref, wts_ref):
    x = x_ref[...] + g1_ref[...] * _dot(y_ref[...], w_ref[...])
    xo_ref[...] = x
    h = _norm_mod(x, gn_ref[...], sc_ref[...], sh_ref[...])
    h_ref[...] = h
    wr = wr_ref[...]
    r_hi, r_mid, r_lo = [_dot(piece, wr) for piece in _split3(h)]
    to_mid, to_lo = LANES - ROUTE_COLS, LANES - 2 * ROUTE_COLS
    small = r_lo + pltpu.roll(r_mid, to_mid, 1) + pltpu.roll(r_hi, to_lo, 1)
    logits = ((small + r_mid) + pltpu.roll(r_hi, to_mid, 1)) + r_hi + br_ref[...]
    ids, wts = _route(logits)
    ids_ref[...] = ids
    wts_ref[...] = wts


def _outproj_route(y_mix, w_out, x, mod, g_norm2, w_route, b_route, rows, layer):
    n, d = x.shape
    tm = rows.tile(256)
    row_tile = pl.BlockSpec((tm, d), lambda i: (i, 0))
    slab = pl.BlockSpec((tm, LANES), lambda i: (i, 0))
    return pl.pallas_call(
        _outproj_kernel,
        grid=(n // tm,),
        in_specs=[row_tile,
                  pl.BlockSpec((d, d), lambda i: (0, 0)),
                  row_tile,
                  _mod_spec(rows, layer, 2, tm),
                  pl.BlockSpec((None, 1, d), lambda i: (layer, 0, 0)),
                  _mod_spec(rows, layer, 3, tm),
                  _mod_spec(rows, layer, 4, tm),
                  pl.BlockSpec((d, LANES), lambda i: (0, 0)),
                  pl.BlockSpec((1, LANES), lambda i: (0, 0))],
        out_specs=[row_tile, row_tile, slab, slab],
        out_shape=[jax.ShapeDtypeStruct((n, d), F32), jax.ShapeDtypeStruct((n, d), F32),
                   jax.ShapeDtypeStruct((n, LANES), jnp.int32), jax.ShapeDtypeStruct((n, LANES), F32)],
        compiler_params=_cp(1),
    )(y_mix, w_out, x, mod, g_norm2.reshape(-1, 1, d), mod, mod, w_route, b_route)


def _row_copy(src_hbm, dst, sem, r, src_row):
    return pltpu.make_async_copy(src_hbm.at[pl.ds(src_row, 1)], dst.at[pl.ds(r, 1)], sem)


def _gather_start(idx_ref, n_rows, src_hbm, dst, sem):
    def body(r, carry):
        _row_copy(src_hbm, dst, sem, r, idx_ref[0, 0, r]).start()
        return carry

    lax.fori_loop(0, n_rows, body, 0, unroll=True)


def _gather_wait(n_rows, src_hbm, dst, sem):
    def body(r, carry):
        _row_copy(src_hbm, dst, sem, r, 0).wait()
        return carry

    lax.fori_loop(0, n_rows, body, 0, unroll=True)


def _moe_kernel(tile_e_ref, n_used_ref, cur_ref, nxt_ref, h_hbm, w1_ref, w3_ref, w2_ref, o_ref,
                xbuf, sem, w1_bf, w3_bf, w2_bf):
    i = pl.program_id(0)
    n_used = n_used_ref[0]
    slot = lax.rem(i, 2)
    tm = xbuf.shape[1]

    new_expert = (i == 0) | (tile_e_ref[i] != tile_e_ref[jnp.maximum(i - 1, 0)])

    @pl.when((i < n_used) & new_expert)
    def _():
        w1_bf[...] = w1_ref[...].astype(BF16)
        w3_bf[...] = w3_ref[...].astype(BF16)
        w2_bf[...] = w2_ref[...].astype(BF16)

    @pl.when(i == 0)
    def _():
        _gather_start(cur_ref, tm, h_hbm, xbuf.at[0], sem.at[0])

    @pl.when(i + 1 < n_used)
    def _():
        _gather_start(nxt_ref, tm, h_hbm, xbuf.at[1 - slot], sem.at[1 - slot])

    @pl.when(i < n_used)
    def _():
        _gather_wait(tm, h_hbm, xbuf.at[slot], sem.at[slot])
        x = xbuf[slot].astype(BF16)
        hid = _silu(_dot(x, w1_bf[...])) * _dot(x, w3_bf[...])
        o_ref[...] = _dot(hid.astype(BF16), w2_bf[...])

    @pl.when(i >= n_used)
    def _():
        o_ref[...] = jnp.zeros(o_ref.shape, F32)


def _moe_experts(h2, tile_expert, n_used, src_rows, w1, w3, w2, layer):
    n, d = h2.shape
    n_tiles = tile_expert.shape[0]
    tm = MOE_TM
    r = n_tiles * tm
    idx_spec = lambda step: pl.BlockSpec(
        (1, 1, tm), lambda i, te, nu: (jnp.minimum(i + step, n_tiles - 1), 0, 0), memory_space=pltpu.SMEM)
    w_spec = lambda rows_, cols_: pl.BlockSpec((None, None, rows_, cols_),
                                               lambda i, te, nu: (layer, te[i], 0, 0))
    grid_spec = pltpu.PrefetchScalarGridSpec(
        num_scalar_prefetch=2,
        grid=(n_tiles,),
        in_specs=[idx_spec(0), idx_spec(1),
                  pl.BlockSpec(memory_space=pl.ANY),
                  w_spec(d, D_EXPERT), w_spec(d, D_EXPERT), w_spec(D_EXPERT, d)],
        out_specs=pl.BlockSpec((tm, d), lambda i, te, nu: (i, 0)),
        scratch_shapes=[pltpu.VMEM((2, tm, d), F32), pltpu.SemaphoreType.DMA((2,)),
                        pltpu.VMEM((d, D_EXPERT), BF16), pltpu.VMEM((d, D_EXPERT), BF16),
                        pltpu.VMEM((D_EXPERT, d), BF16)],
    )
    src3 = src_rows.reshape(n_tiles, 1, tm)
    return pl.pallas_call(
        _moe_kernel,
        grid_spec=grid_spec,
        out_shape=jax.ShapeDtypeStruct((r, d), F32),
        compiler_params=_cp(1),
    )(tile_expert, n_used, src3, src3, h2, w1, w3, w2)


def _plan_routing(ids):
    n = ids.shape[0]
    tm = MOE_TM
    n_tiles = (2 * n) // tm + N_EXPERTS
    r = n_tiles * tm
    e_flat = ids[:, :2].reshape(-1)
    onehot = (e_flat[:, None] == jnp.arange(N_EXPERTS, dtype=jnp.int32)[None, :]).astype(jnp.int32)
    csum = jnp.cumsum(onehot, axis=0)
    rank = jnp.sum(onehot * csum, axis=1) - 1
    counts = csum[-1]
    padded = ((counts + tm - 1) // tm) * tm
    ends = jnp.cumsum(padded)
    offs = ends - padded
    dest = jnp.sum(onehot * offs[None, :], axis=1) + rank
    tile_start = jnp.arange(n_tiles, dtype=jnp.int32) * tm
    tile_expert = jnp.minimum(jnp.sum((tile_start[:, None] >= ends[None, :]).astype(jnp.int32), axis=1),
                              N_EXPERTS - 1).astype(jnp.int32)
    tok = jnp.arange(2 * n, dtype=jnp.int32) // 2
    src_rows = jnp.zeros((r,), jnp.int32).at[dest].set(tok)
    n_used = (ends[-1:] // tm).astype(jnp.int32)
    return tile_expert, n_used, src_rows, dest.reshape(n, 2)


def _combine_kernel(cur_ref, nxt_ref, x_ref, wts_ref, g2_ref, gn_ref, sh_ref, sc_ref, ys_hbm,
                    xo_ref, h_ref, buf, sem, *, final):
    i = pl.program_id(0)
    slot = lax.rem(i, 2)
    tm = x_ref.shape[0]

    @pl.when(i == 0)
    def _():
        _gather_start(cur_ref, 2 * tm, ys_hbm, buf.at[0], sem.at[0])

    @pl.when(i + 1 < pl.num_programs(0))
    def _():
        _gather_start(nxt_ref, 2 * tm, ys_hbm, buf.at[1 - slot], sem.at[1 - slot])

    _gather_wait(2 * tm, ys_hbm, buf.at[slot], sem.at[slot])
    wts = wts_ref[...]
    moe = buf[slot, pl.ds(0, tm), :] * wts[:, 0:1] + buf[slot, pl.ds(tm, tm), :] * wts[:, 1:2]
    x = x_ref[...] + g2_ref[...] * moe
    xo_ref[...] = x
    if final:
        ms = jnp.mean(x * x, axis=-1, keepdims=True)
        h_ref[...] = x * lax.rsqrt(ms + EPS) * gn_ref[...]
    else:
        h_ref[...] = _norm_mod(x, gn_ref[...], sc_ref[...], sh_ref[...]).astype(BF16)


def _combine(ys, dest, wts, x, mod, g_next, rows, layer, final):
    n, d = x.shape
    tm = rows.tile(256)
    n_tiles = n // tm
    dest3 = dest.reshape(n_tiles, tm, 2).transpose(0, 2, 1).reshape(n_tiles, 1, 2 * tm)
    idx_spec = lambda step: pl.BlockSpec(
        (1, 1, 2 * tm), lambda i: (jnp.minimum(i + step, n_tiles - 1), 0, 0), memory_space=pltpu.SMEM)
    row_tile = pl.BlockSpec((tm, d), lambda i: (i, 0))
    nxt = layer if final else layer + 1
    gain_spec = (pl.BlockSpec((1, d), lambda i: (0, 0)) if final
                 else pl.BlockSpec((None, 1, d), lambda i: (nxt, 0, 0)))
    gain = g_next.reshape(1, d) if final else g_next.reshape(-1, 1, d)
    return pl.pallas_call(
        functools.partial(_combine_kernel, final=final),
        grid=(n_tiles,),
        in_specs=[idx_spec(0), idx_spec(1),
                  row_tile,
                  pl.BlockSpec((tm, LANES), lambda i: (i, 0)),
                  _mod_spec(rows, layer, 5, tm),
                  gain_spec,
                  _mod_spec(rows, nxt, 0, tm),
                  _mod_spec(rows, nxt, 1, tm),
                  pl.BlockSpec(memory_space=pl.ANY)],
        out_specs=[row_tile, row_tile],
        out_shape=[jax.ShapeDtypeStruct((n, d), F32), jax.ShapeDtypeStruct((n, d), F32 if final else BF16)],
        scratch_shapes=[pltpu.VMEM((2, 2 * tm, d), F32), pltpu.SemaphoreType.DMA((2,))],
        compiler_params=_cp(1),
    )(dest3, dest3, x, wts, mod, gain, mod, mod, ys)


def kernel(x_prompt, x_sample, state_ret, state_hgrn, state_rglru, c, c_ctx, w_ada, b_ada, g_norm1, g_norm2, w_even_in, w_even_out, ret_decay, hg_lb_logits, g_ret_head, g_hg_head, w_odd_in, conv_w, conv_b, w_a, b_a, w_x, b_x, rg_lambda, w_odd_out, w_group, b_group, w_router, b_router, w1, w3, w2, g_final):
    bp, tp, d = x_prompt.shape
    bs, ts, _ = x_sample.shape
    depth = w_ada.shape[0]
    rows = _Rows(bp, tp, bs, ts)
    assert bs + 1 <= COND_ROWS

    lb_sm = jax.nn.softmax(hg_lb_logits.astype(F32), axis=0)
    lb_all = jnp.cumsum(lb_sm, axis=0) - lb_sm[0:1]
    log_gamma = -jnp.exp(ret_decay.astype(F32))
    softplus_neg_lam = jax.nn.softplus(-rg_lambda.astype(F32))
    t_idx = jnp.arange(ts)
    freqs = ROPE_BASE ** (-jnp.arange(HEAD_DIM // 4, dtype=F32) / (HEAD_DIM // 4))
    ang = jnp.concatenate([(t_idx // GRID_W).astype(F32)[:, None] * freqs,
                           (t_idx % GRID_W).astype(F32)[:, None] * freqs], axis=-1)
    rope_tabs = (jnp.concatenate([jnp.cos(ang), jnp.cos(ang)], axis=-1),
                 jnp.concatenate([-jnp.sin(ang), jnp.sin(ang)], axis=-1))
    w_route = jnp.concatenate(
        list(_split3(jnp.concatenate([w_group, w_router], axis=-1).astype(F32)))
        + [jnp.zeros((depth, d, LANES - 3 * ROUTE_COLS), BF16)], axis=-1)
    b_route = jnp.concatenate(
        [b_group, b_router, jnp.zeros((depth, LANES - ROUTE_COLS), F32)], axis=-1)
    bf = lambda w: w.astype(BF16)

    cond = jnp.zeros((COND_ROWS, d), F32).at[:bs].set(c).at[bs].set(c_ctx)
    mod = _adaln_all(cond, w_ada, b_ada)

    x = jnp.concatenate([x_prompt.reshape(bp * tp, d), x_sample.reshape(bs * ts, d)], axis=0)
    h = _norm0(x, g_norm1, mod, rows, 0)
    new_ret, new_hg, new_rg = [], [], []
    for l in range(depth):
        y_mix = jnp.zeros((rows.n, d), BF16)
        if l % 2 == 0:
            e = l // 2
            proj = _in_proj(h, bf(w_even_in[e]))
            y_mix, sr = _retention(proj, y_mix, log_gamma[e], g_ret_head[e], 0, bp, tp, None, None, True)
            y_mix, _ = _retention(proj, y_mix, log_gamma[e], g_ret_head[e], rows.n_ctx, bs, ts, rope_tabs,
                                  state_ret[:, e], False)
            y_mix, sh = _hgrn2(proj, y_mix, lb_all[e], g_hg_head[e], 0, bp, tp, None, True)
            y_mix, _ = _hgrn2(proj, y_mix, lb_all[e], g_hg_head[e], rows.n_ctx, bs, ts, state_hgrn[:, e], False)
            new_ret.append(sr)
            new_hg.append(sh)
            w_out = bf(w_even_out[e])
        else:
            o = l // 2
            proj = _in_proj(h, bf(w_odd_in[o]))
            wa, wx = bf(w_a[o]), bf(w_x[o])
            y_mix, sg = _rglru(proj, y_mix, conv_w[o], conv_b[o], wa, wx, b_a[o], b_x[o], softplus_neg_lam[o],
                               0, bp, tp, None, True)
            y_mix, _ = _rglru(proj, y_mix, conv_w[o], conv_b[o], wa, wx, b_a[o], b_x[o], softplus_neg_lam[o],
                              rows.n_ctx, bs, ts, state_rglru[:, o], False)
            new_rg.append(sg)
            w_out = bf(w_odd_out[o])
        x, h2, ids, wts = _outproj_route(y_mix, w_out, x, mod, g_norm2, w_route[l], b_route[l:l + 1], rows, l)
        tile_expert, n_used, src_rows, dest = _plan_routing(ids)
        ys = _moe_experts(h2, tile_expert, n_used, src_rows, w1, w3, w2, l)
        final = l == depth - 1
        x, h = _combine(ys, dest, wts, x, mod, g_final if final else g_norm1, rows, l, final)

    y_prompt = h[:rows.n_ctx].reshape(bp, tp, d)
    y_sample = h[rows.n_ctx:].reshape(bs, ts, d)
    return (y_prompt, y_sample, jnp.stack(new_ret, axis=1), jnp.stack(new_hg, axis=1),
            jnp.stack(new_rg, axis=1))
```

```python
import functools

import numpy as np
import jax
import jax.numpy as jnp
from jax import lax
from jax.experimental import pallas as pl
from jax.experimental.pallas import tpu as pltpu

F32 = jnp.float32
BF16 = jnp.bfloat16
HIGHEST = lax.Precision.HIGHEST

D_MODEL = 2048
GRID_W = 64
HEAD_DIM = 128
N_HEADS = 8
MIX_HALF = N_HEADS * HEAD_DIM
EVEN_IN = 9 * MIX_HALF
ROPE_BASE = 10000.0
RG_BLOCK = 128
RG_C = 8.0
N_GROUPS = 4
EXPERTS_PER_GROUP = 4
N_EXPERTS = N_GROUPS * EXPERTS_PER_GROUP
D_EXPERT = 512
EPS = 1e-6
F_MIN = 1e-20
COND_ROWS = 16
LANES = 128
SUBLANES = 8

RET_CHUNK = 128
HG_CHUNK = 128
HG_UNROLL = 2
HG_HEADS_PER_STEP = 2
HG_SPLIT = 2
ODD_CB = 512
ROUTE_COLS = N_GROUPS + N_EXPERTS
MOE_TM = 256
VMEM_LIMIT = 52 * 1024 * 1024


def _cp(n_axes, vmem=VMEM_LIMIT):
    return pltpu.CompilerParams(dimension_semantics=("arbitrary",) * n_axes, vmem_limit_bytes=vmem)


def _silu(x):
    return x * jax.nn.sigmoid(x)


def _sigmoid_tanh(x):
    return 0.5 + 0.5 * jnp.tanh(0.5 * x)


def _dot(a, b, **kw):
    return jnp.dot(a, b, preferred_element_type=F32, **kw)


def _dot_nt(a, b):
    return lax.dot_general(a, b, (((1,), (1,)), ((), ())), preferred_element_type=F32)


def _dot_tn(a, b):
    return lax.dot_general(a, b, (((0,), (0,)), ((), ())), preferred_element_type=F32)


def _norm_mod(x, gain, scale, shift):
    ms = jnp.mean(x * x, axis=-1, keepdims=True)
    return x * lax.rsqrt(ms + EPS) * gain * (1.0 + scale) + shift


def _ada_kernel(cond_ref, w_ref, b_ref, o_ref):
    s3 = jnp.concatenate(_split3(_silu(cond_ref[...])), axis=0)
    w_hi, w_mid, _ = _split3(w_ref[...])
    r_hi, r_mid = _dot(s3, w_hi), _dot(s3, w_mid)
    n = COND_ROWS
    o_ref[...] = ((r_mid[0:n] + r_mid[n:2 * n] + r_hi[2 * n:3 * n])
                  + r_hi[n:2 * n] + r_hi[0:n] + b_ref[...])


def _adaln_all(cond, w_ada, b_ada):
    depth, d, n6 = w_ada.shape
    tn = 1024
    out = pl.pallas_call(
        _ada_kernel,
        grid=(depth, n6 // tn),
        in_specs=[pl.BlockSpec((COND_ROWS, d), lambda l, j: (0, 0)),
                  pl.BlockSpec((None, d, tn), lambda l, j: (l, 0, j)),
                  pl.BlockSpec((None, 1, tn), lambda l, j: (l, 0, j))],
        out_specs=pl.BlockSpec((None, COND_ROWS, tn), lambda l, j: (l, 0, j)),
        out_shape=jax.ShapeDtypeStruct((depth, COND_ROWS, n6), F32),
        compiler_params=_cp(2),
    )(cond, w_ada, b_ada.reshape(depth, 1, n6))
    return out.reshape(depth, COND_ROWS, 6, 1, d)


class _Rows:
    def __init__(self, bp, tp, bs, ts):
        self.bp, self.tp, self.bs, self.ts = bp, tp, bs, ts
        self.n_ctx = bp * tp
        self.n = self.n_ctx + bs * ts
        self.ctx_row = bs

    def cond_row(self, i, tm):
        n_ctx_tiles = self.n_ctx // tm
        per_seq = self.ts // tm
        return jnp.where(i < n_ctx_tiles, self.ctx_row, (i - n_ctx_tiles) // per_seq)

    def tile(self, cap):
        tm = min(cap, self.ts)
        assert self.ts % tm == 0 and self.n_ctx % tm == 0
        return tm


def _mod_spec(rows, layer, which, tm):
    d = D_MODEL
    return pl.BlockSpec((None, None, None, 1, d),
                        lambda i, *_: (layer, rows.cond_row(i, tm), which, 0, 0))


def _norm0_kernel(x_ref, g_ref, sh_ref, sc_ref, h_ref):
    h_ref[...] = _norm_mod(x_ref[...], g_ref[...], sc_ref[...], sh_ref[...]).astype(BF16)


def _norm0(x, g_norm, mod, rows, layer):
    n, d = x.shape
    tm = rows.tile(512)
    return pl.pallas_call(
        _norm0_kernel,
        grid=(n // tm,),
        in_specs=[pl.BlockSpec((tm, d), lambda i: (i, 0)),
                  pl.BlockSpec((None, 1, d), lambda i: (layer, 0, 0)),
                  _mod_spec(rows, layer, 0, tm),
                  _mod_spec(rows, layer, 1, tm)],
        out_specs=pl.BlockSpec((tm, d), lambda i: (i, 0)),
        out_shape=jax.ShapeDtypeStruct((n, d), BF16),
        compiler_params=_cp(1),
    )(x, g_norm.reshape(-1, 1, d), mod, mod)


def _matmul_kernel(h_ref, w_ref, o_ref):
    o_ref[...] = _dot(h_ref[...], w_ref[...].astype(BF16))


def _in_proj(h, w, layer):
    n, k = h.shape
    _, _, nout = w.shape
    tm = 2048 if n % 2048 == 0 else 128
    tn = 512
    return pl.pallas_call(
        _matmul_kernel,
        grid=(n // tm, nout // tn),
        in_specs=[pl.BlockSpec((tm, k), lambda i, j: (i, 0)),
                  pl.BlockSpec((None, k, tn), lambda i, j: (layer, 0, j))],
        out_specs=pl.BlockSpec((tm, tn), lambda i, j: (i, j)),
        out_shape=jax.ShapeDtypeStruct((n, nout), F32),
        compiler_params=_cp(2),
    )(h, w)


def _ret_kernel(*refs, t, rope, has_state, emit_state):
    refs = list(refs)
    lg_ref = refs.pop(0)
    q_ref, k_ref, v_ref, g_ref, gain_ref = [refs.pop(0) for _ in range(5)]
    if rope:
        cos_ref, sin_ref = refs.pop(0), refs.pop(0)
    if has_state:
        s0_ref = refs.pop(0)
    refs.pop(0)
    o_ref = refs.pop(0)
    if emit_state:
        st_ref = refs.pop(0)
    sb_scr, qs, ks = refs

    c = min(RET_CHUNK, t)
    n_chunks = t // c
    head = pl.program_id(1)
    lgf = lg_ref[0, head]
    lgb = lg_ref[1, head]

    q = q_ref[...]
    k = k_ref[...]
    if rope:
        cs, sn = cos_ref[...], sin_ref[...]
        q = q * cs + pltpu.roll(q, HEAD_DIM // 2, 1) * sn
        k = k * cs + pltpu.roll(k, HEAD_DIM // 2, 1) * sn
    qs[...] = q
    ks[...] = k * (HEAD_DIM ** -0.5)

    pos_c = lax.broadcasted_iota(jnp.int32, (c, 1), 0).astype(F32)
    pos_r = lax.broadcasted_iota(jnp.int32, (1, c), 1).astype(F32)
    rel = pos_c - pos_r
    decay = (jnp.where(rel >= 0, jnp.exp(lgf * jnp.maximum(rel, 0.0)), 0.0)
             + jnp.where(rel <= 0, jnp.exp(lgb * jnp.maximum(-rel, 0.0)), 0.0))
    qdf = jnp.exp(lgf * (pos_c + 1.0))
    kdf = jnp.exp(lgf * (c - 1.0 - pos_c))
    qdb = jnp.exp(lgb * (c - pos_c))
    kdb = jnp.exp(lgb * pos_c)
    full = jnp.full((1, HEAD_DIM), float(c), F32)
    cdf = jnp.exp(lgf * full)
    cdb = jnp.exp(lgb * full)

    def rows_of(i):
        return pl.ds(pl.multiple_of(i * c, c), c)

    def bwd_body(ii, s):
        i = n_chunks - 1 - ii
        sb_scr[i] = s
        sl = rows_of(i)
        kv = _dot_tn((ks[sl, :] * kdb).astype(BF16), v_ref[sl, :].astype(BF16))
        return s * cdb + kv

    s0_b = s0_ref[1] if has_state else jnp.zeros((HEAD_DIM, HEAD_DIM), F32)
    s_b = lax.fori_loop(0, n_chunks, bwd_body, s0_b, unroll=True)

    gain = gain_ref[...]

    def fwd_body(i, s):
        sl = rows_of(i)
        qc, kc = qs[sl, :], ks[sl, :]
        vb = v_ref[sl, :].astype(BF16)
        scores = _dot_nt(qc.astype(BF16), kc.astype(BF16)) * decay
        o = (_dot(scores.astype(BF16), vb)
             + _dot((qc * qdf).astype(BF16), s.astype(BF16))
             + _dot((qc * qdb).astype(BF16), sb_scr[i].astype(BF16)))
        ms = jnp.mean(o * o, axis=-1, keepdims=True)
        o = o * lax.rsqrt(ms + EPS) * gain
        o_ref[sl, :] = (o * _silu(g_ref[sl, :])).astype(BF16)
        return s * cdf + _dot_tn((kc * kdf).astype(BF16), vb)

    s0_f = s0_ref[0] if has_state else jnp.zeros((HEAD_DIM, HEAD_DIM), F32)
    s_f = lax.fori_loop(0, n_chunks, fwd_body, s0_f, unroll=True)
    if emit_state:
        st_ref[0] = s_f
        st_ref[1] = s_b


def _retention(proj, y_mix, log_gamma, g_head, row0, b, t, rope_tabs, s0, emit_state):
    rb0 = row0 // t
    blk = lambda col0: pl.BlockSpec((t, HEAD_DIM), lambda bi, h: (rb0 + bi, col0 + h))
    in_specs = [pl.BlockSpec(memory_space=pltpu.SMEM),
                blk(0), blk(N_HEADS), blk(2 * N_HEADS), blk(3 * N_HEADS),
                pl.BlockSpec((None, 1, HEAD_DIM), lambda bi, h: (h, 0, 0))]
    args = [log_gamma, proj, proj, proj, proj, g_head.reshape(N_HEADS, 1, HEAD_DIM)]
    if rope_tabs is not None:
        in_specs += [pl.BlockSpec((t, HEAD_DIM), lambda bi, h: (0, 0))] * 2
        args += list(rope_tabs)
    if s0 is not None:
        in_specs.append(pl.BlockSpec((None, 2, None, HEAD_DIM, HEAD_DIM), lambda bi, h: (bi, 0, h, 0, 0)))
        args.append(s0)
    in_specs.append(pl.BlockSpec(memory_space=pl.ANY))
    args.append(y_mix)
    alias_idx = len(args) - 1
    out_specs = [pl.BlockSpec((t, HEAD_DIM), lambda bi, h: (rb0 + bi, h))]
    out_shape = [jax.ShapeDtypeStruct(y_mix.shape, y_mix.dtype)]
    if emit_state:
        out_specs.append(pl.BlockSpec((None, 2, None, HEAD_DIM, HEAD_DIM), lambda bi, h: (bi, 0, h, 0, 0)))
        out_shape.append(jax.ShapeDtypeStruct((b, 2, N_HEADS, HEAD_DIM, HEAD_DIM), F32))
    n_chunks = t // min(RET_CHUNK, t)
    outs = pl.pallas_call(
        functools.partial(_ret_kernel, t=t, rope=rope_tabs is not None, has_state=s0 is not None,
                          emit_state=emit_state),
        grid=(b, N_HEADS),
        in_specs=in_specs,
        out_specs=out_specs,
        out_shape=out_shape,
        scratch_shapes=[pltpu.VMEM((n_chunks, HEAD_DIM, HEAD_DIM), F32),
                        pltpu.VMEM((t, HEAD_DIM), F32),
                        pltpu.VMEM((t, HEAD_DIM), F32)],
        input_output_aliases={alias_idx: 0},
        compiler_params=_cp(2),
    )(*args)
    return outs[0], (outs[1] if emit_state else None)


def _hg_plan(c):
    t = np.arange(c)
    lower = (t[:, None] >= t[None, :]).astype(np.float32)
    stacks, masks = [], []
    for fwd in (True, False):
        tri = lower if fwd else lower.T
        end = c - 1 if fwd else 0
        blocks = [tri, tri[end:end + 1, :] - tri]
        level_masks = []
        w = 1
        while w < c:
            blk = t // (2 * w)
            late = (t % (2 * w)) >= w
            query = late if fwd else ~late
            ref = blk * 2 * w + (w - 1 if fwd else w)
            blocks.append(np.where(query, 1.0, -1.0)[:, None] * (tri - tri[ref, :]))
            level_masks.append(((blk[:, None] == blk[None, :]) & query[:, None] & ~query[None, :])
                               .astype(np.float32))
            w *= 2
        m = np.concatenate(blocks, axis=0)
        stacks.append(np.concatenate([m] * HG_SPLIT, axis=1))
        masks.append(np.stack(level_masks))
    return np.stack(stacks), np.stack(masks)


def _split3(x):
    hi = x.astype(BF16)
    r1 = x - hi.astype(F32)
    mid = r1.astype(BF16)
    lo = (r1 - mid.astype(F32)).astype(BF16)
    return hi, mid, lo


def _hg_kernel(*refs, t, has_state, emit_state):
    refs = list(refs)
    (q_ref, ff_ref, fb_ref, v_ref, gate_ref, lb_ref, gain_ref, m_ref, mask_ref) = [refs.pop(0) for _ in range(9)]
    if has_state:
        s0_ref = refs.pop(0)
    refs.pop(0)
    o_ref = refs.pop(0)
    if emit_state:
        st_ref = refs.pop(0)
    qa, k_fw, l_fw, k_bw, l_bw, o_fw, o_bw, st = refs

    c = min(HG_CHUNK, t)
    n_chunks = t // c
    n_levels = mask_ref.shape[1]
    hp = HG_HEADS_PER_STEP
    heads = [slice(hh * HEAD_DIM, (hh + 1) * HEAD_DIM) for hh in range(hp)]
    lbh = lb_ref[...]
    qa[...] = _silu(q_ref[...])

    def forget(z_ref, k_out, l_out):
        sg = jax.nn.sigmoid(z_ref[...])
        l_out[...] = jnp.log(jnp.maximum(lbh + (1.0 - lbh) * sg, F_MIN))
        k_out[...] = (1.0 - lbh) * (1.0 - sg)

    forget(ff_ref, k_fw, l_fw)
    forget(fb_ref, k_bw, l_bw)

    for d in range(2):
        for hh in range(hp):
            st[d, hh] = s0_ref[d, hh].T if has_state else jnp.zeros((HEAD_DIM, HEAD_DIM), F32)

    row = lax.broadcasted_iota(jnp.int32, (c, c), 0)
    col = lax.broadcasted_iota(jnp.int32, (c, c), 1)

    def rows_of(i):
        return pl.ds(pl.multiple_of(i * c, c), c)

    def direction(d, sl, own):
        q, k, vb = qa[sl, :], (k_fw, k_bw)[d][sl, :], v_ref[sl, :].astype(BF16)
        pieces = _split3((l_fw, l_bw)[d][sl, :])[:HG_SPLIT]
        e = jnp.exp(_dot(m_ref[d], jnp.concatenate(pieces, axis=0)))
        end = c - 1 if d == 0 else 0
        outs = []
        for hh, ln in enumerate(heads):
            qh, kh, vh = q[:, ln], k[:, ln], vb[:, ln]
            p = own[hh] if own is not None else jnp.zeros((c, c), F32)
            for j in range(n_levels):
                ej = e[(2 + j) * c:(3 + j) * c, ln]
                p = p + _dot_nt((qh * ej).astype(BF16), (kh * ej).astype(BF16)) * mask_ref[d, j]
            s = st[d, hh]
            outs.append(_dot(p.astype(BF16), vh) + _dot_nt((qh * e[0:c, ln]).astype(BF16), s.astype(BF16)))
            st[d, hh] = s * e[end:end + 1, ln] + _dot_tn(vh, (kh * e[c:2 * c, ln]).astype(BF16))
        return jnp.concatenate(outs, axis=1)

    def body(i, carry):
        sl = rows_of(i)
        k_both = (k_fw[sl, :] + k_bw[sl, :]).astype(BF16)
        q = qa[sl, :].astype(BF16)
        own = [jnp.where(row == col, _dot_nt(q[:, ln], k_both[:, ln]), 0.0) for ln in heads]
        o_fw[sl, :] = direction(0, sl, own)
        sl = rows_of(n_chunks - 1 - i)
        o_bw[sl, :] = direction(1, sl, None)
        return carry

    lax.fori_loop(0, n_chunks, body, 0, unroll=min(HG_UNROLL, n_chunks))
    gate = _silu(gate_ref[...])
    gain = gain_ref[...]
    for ln in heads:
        o = o_fw[:, ln] + o_bw[:, ln]
        ms = jnp.mean(o * o, axis=-1, keepdims=True)
        o_ref[:, ln] = (o * lax.rsqrt(ms + EPS) * gain[:, ln] * gate[:, ln]).astype(BF16)
    if emit_state:
        for d in range(2):
            for hh in range(hp):
                st_ref[d, hh] = st[d, hh].T


def _hgrn2(proj, y_mix, lb, g_head, row0, b, t, s0, emit_state):
    rb0 = row0 // t
    c = min(HG_CHUNK, t)
    hp = HG_HEADS_PER_STEP
    lanes = hp * HEAD_DIM
    m_np, mask_np = _hg_plan(c)
    m_stack = jnp.asarray(m_np, BF16)
    lvl_mask = jnp.asarray(mask_np, F32)
    hblocks = N_HEADS // hp
    blk = lambda col0: pl.BlockSpec((t, lanes), lambda bi, h: (rb0 + bi, col0 + h))
    per_head = pl.BlockSpec((None, 1, lanes), lambda bi, h: (h, 0, 0))
    state_spec = pl.BlockSpec((None, 2, hp, HEAD_DIM, HEAD_DIM), lambda bi, h: (bi, 0, h, 0, 0))
    in_specs = [blk(4 * hblocks), blk(5 * hblocks), blk(6 * hblocks), blk(7 * hblocks), blk(8 * hblocks),
                per_head, per_head,
                pl.BlockSpec(m_stack.shape, lambda bi, h: (0, 0, 0)),
                pl.BlockSpec(lvl_mask.shape, lambda bi, h: (0, 0, 0, 0))]
    args = [proj] * 5 + [lb.reshape(hblocks, 1, lanes), g_head.reshape(hblocks, 1, lanes),
                         m_stack, lvl_mask]
    if s0 is not None:
        in_specs.append(state_spec)
        args.append(s0)
    in_specs.append(pl.BlockSpec(memory_space=pl.ANY))
    args.append(y_mix)
    alias_idx = len(args) - 1
    out_specs = [pl.BlockSpec((t, lanes), lambda bi, h: (rb0 + bi, hblocks + h))]
    out_shape = [jax.ShapeDtypeStruct(y_mix.shape, y_mix.dtype)]
    if emit_state:
        out_specs.append(state_spec)
        out_shape.append(jax.ShapeDtypeStruct((b, 2, N_HEADS, HEAD_DIM, HEAD_DIM), F32))
    seq = pltpu.VMEM((t, lanes), F32)
    outs = pl.pallas_call(
        functools.partial(_hg_kernel, t=t, has_state=s0 is not None, emit_state=emit_state),
        grid=(b, hblocks),
        in_specs=in_specs,
        out_specs=out_specs,
        out_shape=out_shape,
        scratch_shapes=[seq] * 7 + [pltpu.VMEM((2, hp, HEAD_DIM, HEAD_DIM), F32)],
        input_output_aliases={alias_idx: 0},
        compiler_params=_cp(2),
    )(*args)
    return outs[0], (outs[1] if emit_state else None)


def _block_scan(a, b, reverse):
    row = lax.broadcasted_iota(jnp.int32, a.shape, 0)
    k = 1
    while k < SUBLANES:
        shift = (SUBLANES - k) if reverse else k
        valid = (row < SUBLANES - k) if reverse else (row >= k)
        a_prev = pltpu.roll(a, shift, 0)
        b_prev = pltpu.roll(b, shift, 0)
        b = jnp.where(valid, a * b_prev + b, b)
        a = jnp.where(valid, a * a_prev, a)
        k *= 2
    return a, b


def _odd_kernel(*refs, t, has_state, emit_state):
    refs = list(refs)
    (gi_ref, xb_ref, cw_ref, cb_ref, wa_ref, wx_ref, ba_ref, bx_ref, sp_ref) = [refs.pop(0) for _ in range(9)]
    if has_state:
        s0_ref = refs.pop(0)
    refs.pop(0)
    o_ref = refs.pop(0)
    if emit_state:
        st_ref = refs.pop(0)
    xpad, xc, a_f, b_f, a_b, b_b, h_f = refs

    cb = xb_ref.shape[1]
    pad = SUBLANES
    zeros = jnp.zeros((pad, cb), F32)
    xpad[pl.ds(0, pad), :] = zeros
    xpad[pl.ds(pad + t, pad), :] = zeros
    xpad[pl.ds(pad, t), :] = xb_ref[...]
    for n in range(cb // RG_BLOCK):
        cols = pl.ds(n * RG_BLOCK, RG_BLOCK)
        xc[:, cols] = (cb_ref[:, cols]
                       + xpad[pl.ds(pad - 1, t), cols] * cw_ref[0:1, cols]
                       + xpad[pl.ds(pad, t), cols] * cw_ref[1:2, cols]
                       + xpad[pl.ds(pad + 1, t), cols] * cw_ref[2:3, cols]
                       + xpad[pl.ds(pad + 2, t), cols] * cw_ref[3:4, cols])

    rc = min(256, t)

    def gate_body(i, carry):
        sl = pl.ds(pl.multiple_of(i * rc, rc), rc)
        for n in range(cb // RG_BLOCK):
            cols = pl.ds(n * RG_BLOCK, RG_BLOCK)
            x_blk = xc[sl, cols]
            x_bf = x_blk.astype(BF16)
            for d, (a_out, b_out) in enumerate(((a_f, b_f), (a_b, b_b))):
                r = _sigmoid_tanh(_dot(x_bf, wa_ref[d, n]) + ba_ref[d, :, cols])
                gate_i = _sigmoid_tanh(_dot(x_bf, wx_ref[d, n]) + bx_ref[d, :, cols])
                log_a = -RG_C * r * sp_ref[d, :, cols]
                a = jnp.exp(log_a)
                a_out[sl, cols] = a
                z = jnp.maximum(-jnp.tanh(log_a) * (1.0 + a * a), F_MIN)
                b_out[sl, cols] = (z * lax.rsqrt(z)) * (gate_i * x_blk)
        return carry

    lax.fori_loop(0, t // rc, gate_body, 0)

    n_blocks = t // SUBLANES

    def rows_of(j):
        return pl.ds(pl.multiple_of(j * SUBLANES, SUBLANES), SUBLANES)

    def fwd_body(j, h_prev):
        sl = rows_of(j)
        a, b = _block_scan(a_f[sl, :], b_f[sl, :], False)
        h = a * h_prev + b
        h_f[sl, :] = h
        return jnp.broadcast_to(h[SUBLANES - 1:SUBLANES, :], h.shape)

    h0_f = s0_ref[0] if has_state else jnp.zeros((1, cb), F32)
    last_f = lax.fori_loop(0, n_blocks, fwd_body, jnp.broadcast_to(h0_f, (SUBLANES, cb)))

    def bwd_body(jj, h_next):
        sl = rows_of(n_blocks - 1 - jj)
        a, b = _block_scan(a_b[sl, :], b_b[sl, :], True)
        h = a * h_next + b
        gate = jax.nn.gelu(gi_ref[sl, :], approximate=True)
        o_ref[sl, :] = ((h_f[sl, :] + h) * gate).astype(BF16)
        return jnp.broadcast_to(h[0:1, :], h.shape)

    h0_b = s0_ref[1] if has_state else jnp.zeros((1, cb), F32)
    first_b = lax.fori_loop(0, n_blocks, bwd_body, jnp.broadcast_to(h0_b, (SUBLANES, cb)))
    if emit_state:
        st_ref[0] = last_f[0:1, :]
        st_ref[1] = first_b[0:1, :]


def _rglru(proj, y_mix, conv_w, conv_b, w_a, w_x, b_a, b_x, softplus_neg_lam, row0, b, t, s0, emit_state):
    d = D_MODEL
    cb = ODD_CB
    ncb = d // cb
    nrb = cb // RG_BLOCK
    rb0 = row0 // t
    vec = lambda rows: pl.BlockSpec((rows, cb), lambda bi, j: (0, j))
    vec2 = pl.BlockSpec((2, 1, cb), lambda bi, j: (0, 0, j))
    wspec = pl.BlockSpec((2, nrb, RG_BLOCK, RG_BLOCK), lambda bi, j: (0, j, 0, 0))
    in_specs = [pl.BlockSpec((t, cb), lambda bi, j: (rb0 + bi, j)),
                pl.BlockSpec((t, cb), lambda bi, j: (rb0 + bi, ncb + j)),
                vec(4), vec(1), wspec, wspec, vec2, vec2, vec2]
    args = [proj, proj, conv_w, conv_b.reshape(1, d), w_a, w_x,
            b_a.reshape(2, 1, d), b_x.reshape(2, 1, d), softplus_neg_lam.reshape(2, 1, d)]
    if s0 is not None:
        in_specs.append(pl.BlockSpec((None, 2, 1, cb), lambda bi, j: (bi, 0, 0, j)))
        args.append(s0.reshape(b, 2, 1, d))
    in_specs.append(pl.BlockSpec(memory_space=pl.ANY))
    args.append(y_mix)
    alias_idx = len(args) - 1
    out_specs = [pl.BlockSpec((t, cb), lambda bi, j: (rb0 + bi, j))]
    out_shape = [jax.ShapeDtypeStruct(y_mix.shape, y_mix.dtype)]
    if emit_state:
        out_specs.append(pl.BlockSpec((None, 2, 1, cb), lambda bi, j: (bi, 0, 0, j)))
        out_shape.append(jax.ShapeDtypeStruct((b, 2, 1, d), F32))
    seq = pltpu.VMEM((t, cb), F32)
    outs = pl.pallas_call(
        functools.partial(_odd_kernel, t=t, has_state=s0 is not None, emit_state=emit_state),
        grid=(b, ncb),
        in_specs=in_specs,
        out_specs=out_specs,
        out_shape=out_shape,
        scratch_shapes=[pltpu.VMEM((t + 2 * SUBLANES, cb), F32), seq, seq, seq, seq, seq, seq],
        input_output_aliases={alias_idx: 0},
        compiler_params=_cp(2),
    )(*args)
    return outs[0], (outs[1].reshape(b, 2, d) if emit_state else None)


def _route(logits):
    lane = lax.broadcasted_iota(jnp.int32, logits.shape, 1)
    neg = jnp.float32(-jnp.inf)
    big = jnp.int32(LANES)

    def arg_max(vals):
        m = jnp.max(vals, axis=-1, keepdims=True)
        return m, jnp.min(jnp.where(vals == m, lane, big), axis=-1, keepdims=True)

    g_logits = jnp.where(lane < N_GROUPS, logits, neg)
    g_max, g_sel = arg_max(g_logits)
    p_grp = 1.0 / jnp.sum(jnp.exp(g_logits - g_max), axis=-1, keepdims=True)
    lo = N_GROUPS + EXPERTS_PER_GROUP * g_sel
    e_logits = jnp.where((lane >= lo) & (lane < lo + EXPERTS_PER_GROUP), logits, neg)
    v1, i1 = arg_max(e_logits)
    v2, i2 = arg_max(jnp.where(lane == i1, neg, e_logits))
    e2 = jnp.exp(v2 - v1)
    w1 = p_grp / (1.0 + e2)
    w2 = p_grp * e2 / (1.0 + e2)
    ids = jnp.where(lane == 0, i1 - N_GROUPS, jnp.where(lane == 1, i2 - N_GROUPS, 0))
    wts = jnp.where(lane == 0, w1, jnp.where(lane == 1, w2, 0.0))
    return ids, wts


def _outproj_kernel(y_ref, w_ref, x_ref, g1_ref, gn_ref, sh_ref, sc_ref, wr_ref, br_ref,
                    xo_ref, h_ref, ids_ref, wts_ref):
    x = x_ref[...] + g1_ref[...] * _dot(y_ref[...], w_ref[...])
    xo_ref[...] = x
    h = _norm_mod(x, gn_ref[...], sc_ref[...], sh_ref[...])
    h_ref[...] = h
    wr = wr_ref[...]
    r_hi, r_mid, r_lo = [_dot(piece, wr) for piece in _split3(h)]
    to_mid, to_lo = LANES - ROUTE_COLS, LANES - 2 * ROUTE_COLS
    small = r_lo + pltpu.roll(r_mid, to_mid, 1) + pltpu.roll(r_hi, to_lo, 1)
    logits = ((small + r_mid) + pltpu.roll(r_hi, to_mid, 1)) + r_hi + br_ref[...]
    ids, wts = _route(logits)
    ids_ref[...] = ids
    wts_ref[...] = wts


def _outproj_route(y_mix, w_out, x, mod, g_norm2, w_route, b_route, rows, layer):
    n, d = x.shape
    tm = rows.tile(256)
    row_tile = pl.BlockSpec((tm, d), lambda i: (i, 0))
    slab = pl.BlockSpec((tm, LANES), lambda i: (i, 0))
    return pl.pallas_call(
        _outproj_kernel,
        grid=(n // tm,),
        in_specs=[row_tile,
                  pl.BlockSpec((d, d), lambda i: (0, 0)),
                  row_tile,
                  _mod_spec(rows, layer, 2, tm),
                  pl.BlockSpec((None, 1, d), lambda i: (layer, 0, 0)),
                  _mod_spec(rows, layer, 3, tm),
                  _mod_spec(rows, layer, 4, tm),
                  pl.BlockSpec((d, LANES), lambda i: (0, 0)),
                  pl.BlockSpec((1, LANES), lambda i: (0, 0))],
        out_specs=[row_tile, row_tile, slab, slab],
        out_shape=[jax.ShapeDtypeStruct((n, d), F32), jax.ShapeDtypeStruct((n, d), F32),
                   jax.ShapeDtypeStruct((n, LANES), jnp.int32), jax.ShapeDtypeStruct((n, LANES), F32)],
        compiler_params=_cp(1),
    )(y_mix, w_out, x, mod, g_norm2.reshape(-1, 1, d), mod, mod, w_route, b_route)


def _row_copy(src_hbm, dst, sem, r, src_row):
    return pltpu.make_async_copy(src_hbm.at[pl.ds(src_row, 1)], dst.at[pl.ds(r, 1)], sem)


def _gather_start(idx_ref, n_rows, src_hbm, dst, sem):
    for r in range(n_rows):
        _row_copy(src_hbm, dst, sem, r, idx_ref[0, 0, r]).start(priority=r % 2)


def _gather_wait(n_rows, src_hbm, dst, sem):
    def body(r, carry):
        _row_copy(src_hbm, dst, sem, r, 0).wait()
        return carry

    lax.fori_loop(0, n_rows, body, 0, unroll=True)


def _moe_kernel(tile_e_ref, n_used_ref, cur_ref, nxt_ref, h_hbm, w1_ref, w3_ref, w2_ref, o_ref,
                xbuf, sem, w1_bf, w3_bf, w2_bf):
    i = pl.program_id(0)
    n_used = n_used_ref[0]
    slot = lax.rem(i, 2)
    tm = xbuf.shape[1]

    new_expert = (i == 0) | (tile_e_ref[i] != tile_e_ref[jnp.maximum(i - 1, 0)])

    @pl.when((i < n_used) & new_expert)
    def _():
        w1_bf[...] = w1_ref[...].astype(BF16)
        w3_bf[...] = w3_ref[...].astype(BF16)
        w2_bf[...] = w2_ref[...].astype(BF16)

    @pl.when(i == 0)
    def _():
        _gather_start(cur_ref, tm, h_hbm, xbuf.at[0], sem.at[0])

    @pl.when(i + 1 < n_used)
    def _():
        _gather_start(nxt_ref, tm, h_hbm, xbuf.at[1 - slot], sem.at[1 - slot])

    @pl.when(i < n_used)
    def _():
        _gather_wait(tm, h_hbm, xbuf.at[slot], sem.at[slot])
        x = xbuf[slot].astype(BF16)
        hid = _silu(_dot(x, w1_bf[...])) * _dot(x, w3_bf[...])
        o_ref[...] = _dot(hid.astype(BF16), w2_bf[...])

    @pl.when(i >= n_used)
    def _():
        o_ref[...] = jnp.zeros(o_ref.shape, F32)


def _moe_experts(h2, tile_expert, n_used, src_rows, w1, w3, w2, layer):
    n, d = h2.shape
    n_tiles = tile_expert.shape[0]
    tm = MOE_TM
    r = n_tiles * tm
    idx_spec = lambda step: pl.BlockSpec(
        (1, 1, tm), lambda i, te, nu: (jnp.minimum(i + step, n_tiles - 1), 0, 0), memory_space=pltpu.SMEM)
    w_spec = lambda rows_, cols_: pl.BlockSpec((None, None, rows_, cols_),
                                               lambda i, te, nu: (layer, te[i], 0, 0))
    grid_spec = pltpu.PrefetchScalarGridSpec(
        num_scalar_prefetch=2,
        grid=(n_tiles,),
        in_specs=[idx_spec(0), idx_spec(1),
                  pl.BlockSpec(memory_space=pl.ANY),
                  w_spec(d, D_EXPERT), w_spec(d, D_EXPERT), w_spec(D_EXPERT, d)],
        out_specs=pl.BlockSpec((tm, d), lambda i, te, nu: (i, 0)),
        scratch_shapes=[pltpu.VMEM((2, tm, d), F32), pltpu.SemaphoreType.DMA((2,)),
                        pltpu.VMEM((d, D_EXPERT), BF16), pltpu.VMEM((d, D_EXPERT), BF16),
                        pltpu.VMEM((D_EXPERT, d), BF16)],
    )
    src3 = src_rows.reshape(n_tiles, 1, tm)
    return pl.pallas_call(
        _moe_kernel,
        grid_spec=grid_spec,
        out_shape=jax.ShapeDtypeStruct((r, d), F32),
        compiler_params=_cp(1),
    )(tile_expert, n_used, src3, src3, h2, w1, w3, w2)


def _plan_routing(ids):
    n = ids.shape[0]
    tm = MOE_TM
    n_tiles = (2 * n) // tm + N_EXPERTS
    r = n_tiles * tm
    e_flat = ids[:, :2].reshape(-1)
    onehot = (e_flat[:, None] == jnp.arange(N_EXPERTS, dtype=jnp.int32)[None, :]).astype(jnp.int32)
    csum = jnp.cumsum(onehot, axis=0)
    rank = jnp.sum(onehot * csum, axis=1) - 1
    counts = csum[-1]
    padded = ((counts + tm - 1) // tm) * tm
    ends = jnp.cumsum(padded)
    offs = ends - padded
    dest = jnp.sum(onehot * offs[None, :], axis=1) + rank
    tile_start = jnp.arange(n_tiles, dtype=jnp.int32) * tm
    tile_expert = jnp.minimum(jnp.sum((tile_start[:, None] >= ends[None, :]).astype(jnp.int32), axis=1),
                              N_EXPERTS - 1).astype(jnp.int32)
    tok = jnp.arange(2 * n, dtype=jnp.int32) // 2
    src_rows = jnp.zeros((r,), jnp.int32).at[dest].set(tok)
    n_used = (ends[-1:] // tm).astype(jnp.int32)
    return tile_expert, n_used, src_rows, dest.reshape(n, 2)


def _combine_kernel(cur_ref, nxt_ref, x_ref, wts_ref, g2_ref, gn_ref, sh_ref, sc_ref, ys_hbm,
                    out_a, out_b, buf, sem, *, n_ctx_tiles):
    i = pl.program_id(0)
    slot = lax.rem(i, 2)
    tm = x_ref.shape[0]

    @pl.when(i == 0)
    def _():
        _gather_start(cur_ref, 2 * tm, ys_hbm, buf.at[0], sem.at[0])

    @pl.when(i + 1 < pl.num_programs(0))
    def _():
        _gather_start(nxt_ref, 2 * tm, ys_hbm, buf.at[1 - slot], sem.at[1 - slot])

    _gather_wait(2 * tm, ys_hbm, buf.at[slot], sem.at[slot])
    wts = wts_ref[...]
    moe = buf[slot, pl.ds(0, tm), :] * wts[:, 0:1] + buf[slot, pl.ds(tm, tm), :] * wts[:, 1:2]
    x = x_ref[...] + g2_ref[...] * moe
    if n_ctx_tiles is None:
        out_a[...] = x
        out_b[...] = _norm_mod(x, gn_ref[...], sc_ref[...], sh_ref[...]).astype(BF16)
    else:
        ms = jnp.mean(x * x, axis=-1, keepdims=True)
        y = x * lax.rsqrt(ms + EPS) * gn_ref[...]

        @pl.when(i < n_ctx_tiles)
        def _():
            out_a[...] = y

        @pl.when(i >= n_ctx_tiles)
        def _():
            out_b[...] = y


def _combine(ys, dest, wts, x, mod, g_next, rows, layer, final):
    n, d = x.shape
    tm = rows.tile(256)
    n_tiles = n // tm
    dest3 = dest.reshape(n_tiles, tm, 2).transpose(0, 2, 1).reshape(n_tiles, 1, 2 * tm)
    idx_spec = lambda step: pl.BlockSpec(
        (1, 1, 2 * tm), lambda i: (jnp.minimum(i + step, n_tiles - 1), 0, 0), memory_space=pltpu.SMEM)
    row_tile = pl.BlockSpec((tm, d), lambda i: (i, 0))
    nxt = layer if final else layer + 1
    gain_spec = (pl.BlockSpec((1, d), lambda i: (0, 0)) if final
                 else pl.BlockSpec((None, 1, d), lambda i: (nxt, 0, 0)))
    gain = g_next.reshape(1, d) if final else g_next.reshape(-1, 1, d)
    if final:
        n_ctx_tiles = rows.n_ctx // tm
        out_specs = [pl.BlockSpec((tm, d), lambda i: (jnp.minimum(i, n_ctx_tiles - 1), 0)),
                     pl.BlockSpec((tm, d), lambda i: (jnp.maximum(i - n_ctx_tiles, 0), 0))]
        out_shape = [jax.ShapeDtypeStruct((rows.n_ctx, d), F32),
                     jax.ShapeDtypeStruct((n - rows.n_ctx, d), F32)]
    else:
        n_ctx_tiles = None
        out_specs = [row_tile, row_tile]
        out_shape = [jax.ShapeDtypeStruct((n, d), F32), jax.ShapeDtypeStruct((n, d), BF16)]
    return pl.pallas_call(
        functools.partial(_combine_kernel, n_ctx_tiles=n_ctx_tiles),
        grid=(n_tiles,),
        in_specs=[idx_spec(0), idx_spec(1),
                  row_tile,
                  pl.BlockSpec((tm, LANES), lambda i: (i, 0)),
                  _mod_spec(rows, layer, 5, tm),
                  gain_spec,
                  _mod_spec(rows, nxt, 0, tm),
                  _mod_spec(rows, nxt, 1, tm),
                  pl.BlockSpec(memory_space=pl.ANY)],
        out_specs=out_specs,
        out_shape=out_shape,
        scratch_shapes=[pltpu.VMEM((2, 2 * tm, d), F32), pltpu.SemaphoreType.DMA((2,))],
        compiler_params=_cp(1),
    )(dest3, dest3, x, wts, mod, gain, mod, mod, ys)


def kernel(x_prompt, x_sample, state_ret, state_hgrn, state_rglru, c, c_ctx, w_ada, b_ada, g_norm1, g_norm2, w_even_in, w_even_out, ret_decay, hg_lb_logits, g_ret_head, g_hg_head, w_odd_in, conv_w, conv_b, w_a, b_a, w_x, b_x, rg_lambda, w_odd_out, w_group, b_group, w_router, b_router, w1, w3, w2, g_final):
    bp, tp, d = x_prompt.shape
    bs, ts, _ = x_sample.shape
    depth = w_ada.shape[0]
    rows = _Rows(bp, tp, bs, ts)
    assert bs + 1 <= COND_ROWS

    lb_sm = jax.nn.softmax(hg_lb_logits.astype(F32), axis=0)
    lb_all = jnp.cumsum(lb_sm, axis=0) - lb_sm[0:1]
    log_gamma = -jnp.exp(ret_decay.astype(F32))
    softplus_neg_lam = jax.nn.softplus(-rg_lambda.astype(F32))
    t_idx = jnp.arange(ts)
    freqs = ROPE_BASE ** (-jnp.arange(HEAD_DIM // 4, dtype=F32) / (HEAD_DIM // 4))
    ang = jnp.concatenate([(t_idx // GRID_W).astype(F32)[:, None] * freqs,
                           (t_idx % GRID_W).astype(F32)[:, None] * freqs], axis=-1)
    rope_tabs = (jnp.concatenate([jnp.cos(ang), jnp.cos(ang)], axis=-1),
                 jnp.concatenate([-jnp.sin(ang), jnp.sin(ang)], axis=-1))
    w_route = jnp.concatenate(
        list(_split3(jnp.concatenate([w_group, w_router], axis=-1).astype(F32)))
        + [jnp.zeros((depth, d, LANES - 3 * ROUTE_COLS), BF16)], axis=-1)
    b_route = jnp.concatenate(
        [b_group, b_router, jnp.zeros((depth, LANES - ROUTE_COLS), F32)], axis=-1)
    bf = lambda w: w.astype(BF16)

    cond = jnp.zeros((COND_ROWS, d), F32).at[:bs].set(c).at[bs].set(c_ctx)
    mod = _adaln_all(cond, w_ada, b_ada)

    x = jnp.concatenate([x_prompt.reshape(bp * tp, d), x_sample.reshape(bs * ts, d)], axis=0)
    h = _norm0(x, g_norm1, mod, rows, 0)
    new_ret, new_hg, new_rg = [], [], []
    y_mix = jnp.zeros((rows.n, d), BF16)
    for l in range(depth):
        if l % 2 == 0:
            e = l // 2
            proj = _in_proj(h, w_even_in, e)
            y_mix, sr = _retention(proj, y_mix, log_gamma[e], g_ret_head[e], 0, bp, tp, None, None, True)
            y_mix, _ = _retention(proj, y_mix, log_gamma[e], g_ret_head[e], rows.n_ctx, bs, ts, rope_tabs,
                                  state_ret[:, e], False)
            y_mix, sh = _hgrn2(proj, y_mix, lb_all[e], g_hg_head[e], 0, bp, tp, None, True)
            y_mix, _ = _hgrn2(proj, y_mix, lb_all[e], g_hg_head[e], rows.n_ctx, bs, ts, state_hgrn[:, e], False)
            new_ret.append(sr)
            new_hg.append(sh)
            w_out = bf(w_even_out[e])
        else:
            o = l // 2
            proj = _in_proj(h, w_odd_in, o)
            wa, wx = bf(w_a[o]), bf(w_x[o])
            y_mix, sg = _rglru(proj, y_mix, conv_w[o], conv_b[o], wa, wx, b_a[o], b_x[o], softplus_neg_lam[o],
                               0, bp, tp, None, True)
            y_mix, _ = _rglru(proj, y_mix, conv_w[o], conv_b[o], wa, wx, b_a[o], b_x[o], softplus_neg_lam[o],
                              rows.n_ctx, bs, ts, state_rglru[:, o], False)
            new_rg.append(sg)
            w_out = bf(w_odd_out[o])
        x, h2, ids, wts = _outproj_route(y_mix, w_out, x, mod, g_norm2, w_route[l], b_route[l:l + 1], rows, l)
        tile_expert, n_used, src_rows, dest = _plan_routing(ids)
        ys = _moe_experts(h2, tile_expert, n_used, src_rows, w1, w3, w2, l)
        final = l == depth - 1
        x, h = _combine(ys, dest, wts, x, mod, g_final if final else g_norm1, rows, l, final)

    return (x.reshape(bp, tp, d), h.reshape(bs, ts, d), jnp.stack(new_ret, axis=1),
            jnp.stack(new_hg, axis=1), jnp.stack(new_rg, axis=1))
```

```python
import functools

import numpy as np
import jax
import jax.numpy as jnp
from jax import lax
from jax.experimental import pallas as pl
from jax.experimental.pallas import tpu as pltpu

F32 = jnp.float32
BF16 = jnp.bfloat16
HIGHEST = lax.Precision.HIGHEST

D_MODEL = 2048
GRID_W = 64
HEAD_DIM = 128
N_HEADS = 8
MIX_HALF = N_HEADS * HEAD_DIM
EVEN_IN = 9 * MIX_HALF
ROPE_BASE = 10000.0
RG_BLOCK = 128
RG_C = 8.0
N_GROUPS = 4
EXPERTS_PER_GROUP = 4
N_EXPERTS = N_GROUPS * EXPERTS_PER_GROUP
D_EXPERT = 512
EPS = 1e-6
F_MIN = 1e-20
COND_ROWS = 16
LANES = 128
SUBLANES = 8

RET_CHUNK = 128
HG_CHUNK = 128
HG_UNROLL = 2
HG_HEADS_PER_STEP = 2
HG_SPLIT = 2
ODD_CB = 512
ROUTE_COLS = N_GROUPS + N_EXPERTS
MOE_TM = 256
VMEM_LIMIT = 52 * 1024 * 1024


def _cp(n_axes, vmem=VMEM_LIMIT):
    return pltpu.CompilerParams(dimension_semantics=("arbitrary",) * n_axes, vmem_limit_bytes=vmem)


def _silu(x):
    return x * jax.nn.sigmoid(x)


def _sigmoid_tanh(x):
    return 0.5 + 0.5 * jnp.tanh(0.5 * x)


def _dot(a, b, **kw):
    return jnp.dot(a, b, preferred_element_type=F32, **kw)


def _dot_nt(a, b):
    return lax.dot_general(a, b, (((1,), (1,)), ((), ())), preferred_element_type=F32)


def _dot_tn(a, b):
    return lax.dot_general(a, b, (((0,), (0,)), ((), ())), preferred_element_type=F32)


def _norm_mod(x, gain, scale, shift):
    ms = jnp.mean(x * x, axis=-1, keepdims=True)
    return x * lax.rsqrt(ms + EPS) * gain * (1.0 + scale) + shift


def _ada_kernel(cond_ref, w_ref, b_ref, o_ref):
    s3 = jnp.concatenate(_split3(_silu(cond_ref[...])), axis=0)
    w_hi, w_mid, _ = _split3(w_ref[...])
    r_hi, r_mid = _dot(s3, w_hi), _dot(s3, w_mid)
    n = COND_ROWS
    o_ref[...] = ((r_mid[0:n] + r_mid[n:2 * n] + r_hi[2 * n:3 * n])
                  + r_hi[n:2 * n] + r_hi[0:n] + b_ref[...])


def _adaln_all(cond, w_ada, b_ada):
    depth, d, n6 = w_ada.shape
    tn = 1024
    out = pl.pallas_call(
        _ada_kernel,
        grid=(depth, n6 // tn),
        in_specs=[pl.BlockSpec((COND_ROWS, d), lambda l, j: (0, 0)),
                  pl.BlockSpec((None, d, tn), lambda l, j: (l, 0, j)),
                  pl.BlockSpec((None, 1, tn), lambda l, j: (l, 0, j))],
        out_specs=pl.BlockSpec((None, COND_ROWS, tn), lambda l, j: (l, 0, j)),
        out_shape=jax.ShapeDtypeStruct((depth, COND_ROWS, n6), F32),
        compiler_params=_cp(2),
    )(cond, w_ada, b_ada.reshape(depth, 1, n6))
    return out.reshape(depth, COND_ROWS, 6, 1, d)


class _Rows:
    def __init__(self, bp, tp, bs, ts):
        self.bp, self.tp, self.bs, self.ts = bp, tp, bs, ts
        self.n_ctx = bp * tp
        self.n = self.n_ctx + bs * ts
        self.ctx_row = bs

    def cond_row(self, i, tm):
        n_ctx_tiles = self.n_ctx // tm
        per_seq = self.ts // tm
        return jnp.where(i < n_ctx_tiles, self.ctx_row, (i - n_ctx_tiles) // per_seq)

    def tile(self, cap):
        tm = min(cap, self.ts)
        assert self.ts % tm == 0 and self.n_ctx % tm == 0
        return tm


def _mod_spec(rows, layer, which, tm):
    d = D_MODEL
    return pl.BlockSpec((None, None, None, 1, d),
                        lambda i, *_: (layer, rows.cond_row(i, tm), which, 0, 0))


def _norm0_kernel(x_ref, g_ref, sh_ref, sc_ref, h_ref):
    h_ref[...] = _norm_mod(x_ref[...], g_ref[...], sc_ref[...], sh_ref[...]).astype(BF16)


def _norm0(x, g_norm, mod, rows, layer):
    n, d = x.shape
    tm = rows.tile(512)
    return pl.pallas_call(
        _norm0_kernel,
        grid=(n // tm,),
        in_specs=[pl.BlockSpec((tm, d), lambda i: (i, 0)),
                  pl.BlockSpec((None, 1, d), lambda i: (layer, 0, 0)),
                  _mod_spec(rows, layer, 0, tm),
                  _mod_spec(rows, layer, 1, tm)],
        out_specs=pl.BlockSpec((tm, d), lambda i: (i, 0)),
        out_shape=jax.ShapeDtypeStruct((n, d), BF16),
        compiler_params=_cp(1),
    )(x, g_norm.reshape(-1, 1, d), mod, mod)


def _matmul_kernel(h_ref, w_ref, o_ref):
    o_ref[...] = _dot(h_ref[...], w_ref[...].astype(BF16))


def _in_proj(h, w, layer):
    n, k = h.shape
    _, _, nout = w.shape
    tm = 2048 if n % 2048 == 0 else 128
    tn = 512
    return pl.pallas_call(
        _matmul_kernel,
        grid=(n // tm, nout // tn),
        in_specs=[pl.BlockSpec((tm, k), lambda i, j: (i, 0)),
                  pl.BlockSpec((None, k, tn), lambda i, j: (layer, 0, j))],
        out_specs=pl.BlockSpec((tm, tn), lambda i, j: (i, j)),
        out_shape=jax.ShapeDtypeStruct((n, nout), F32),
        compiler_params=_cp(2),
    )(h, w)


def _ret_kernel(*refs, t, rope, has_state, emit_state):
    refs = list(refs)
    lg_ref = refs.pop(0)
    q_ref, k_ref, v_ref, g_ref, gain_ref = [refs.pop(0) for _ in range(5)]
    if rope:
        cos_ref, sin_ref = refs.pop(0), refs.pop(0)
    if has_state:
        s0_ref = refs.pop(0)
    refs.pop(0)
    o_ref = refs.pop(0)
    if emit_state:
        st_ref = refs.pop(0)
    sb_scr, qs, ks = refs

    c = min(RET_CHUNK, t)
    n_chunks = t // c
    head = pl.program_id(1)
    lgf = lg_ref[0, head]
    lgb = lg_ref[1, head]

    q = q_ref[...]
    k = k_ref[...]
    if rope:
        cs, sn = cos_ref[...], sin_ref[...]
        q = q * cs + pltpu.roll(q, HEAD_DIM // 2, 1) * sn
        k = k * cs + pltpu.roll(k, HEAD_DIM // 2, 1) * sn
    qs[...] = q
    ks[...] = k * (HEAD_DIM ** -0.5)

    pos_c = lax.broadcasted_iota(jnp.int32, (c, 1), 0).astype(F32)
    pos_r = lax.broadcasted_iota(jnp.int32, (1, c), 1).astype(F32)
    rel = pos_c - pos_r
    decay = (jnp.where(rel >= 0, jnp.exp(lgf * jnp.maximum(rel, 0.0)), 0.0)
             + jnp.where(rel <= 0, jnp.exp(lgb * jnp.maximum(-rel, 0.0)), 0.0))
    qdf = jnp.exp(lgf * (pos_c + 1.0))
    kdf = jnp.exp(lgf * (c - 1.0 - pos_c))
    qdb = jnp.exp(lgb * (c - pos_c))
    kdb = jnp.exp(lgb * pos_c)
    full = jnp.full((1, HEAD_DIM), float(c), F32)
    cdf = jnp.exp(lgf * full)
    cdb = jnp.exp(lgb * full)

    def rows_of(i):
        return pl.ds(pl.multiple_of(i * c, c), c)

    def bwd_body(ii, s):
        i = n_chunks - 1 - ii
        sb_scr[i] = s
        sl = rows_of(i)
        kv = _dot_tn((ks[sl, :] * kdb).astype(BF16), v_ref[sl, :].astype(BF16))
        return s * cdb + kv

    s0_b = s0_ref[1] if has_state else jnp.zeros((HEAD_DIM, HEAD_DIM), F32)
    s_b = lax.fori_loop(0, n_chunks, bwd_body, s0_b, unroll=True)

    gain = gain_ref[...]

    def fwd_body(i, s):
        sl = rows_of(i)
        qc, kc = qs[sl, :], ks[sl, :]
        vb = v_ref[sl, :].astype(BF16)
        scores = _dot_nt(qc.astype(BF16), kc.astype(BF16)) * decay
        o = (_dot(scores.astype(BF16), vb)
             + _dot((qc * qdf).astype(BF16), s.astype(BF16))
             + _dot((qc * qdb).astype(BF16), sb_scr[i].astype(BF16)))
        ms = jnp.mean(o * o, axis=-1, keepdims=True)
        o = o * lax.rsqrt(ms + EPS) * gain
        o_ref[sl, :] = (o * _silu(g_ref[sl, :])).astype(BF16)
        return s * cdf + _dot_tn((kc * kdf).astype(BF16), vb)

    s0_f = s0_ref[0] if has_state else jnp.zeros((HEAD_DIM, HEAD_DIM), F32)
    s_f = lax.fori_loop(0, n_chunks, fwd_body, s0_f, unroll=True)
    if emit_state:
        st_ref[0] = s_f
        st_ref[1] = s_b


def _retention(proj, y_mix, log_gamma, g_head, row0, b, t, rope_tabs, s0, emit_state):
    rb0 = row0 // t
    blk = lambda col0: pl.BlockSpec((t, HEAD_DIM), lambda bi, h: (rb0 + bi, col0 + h))
    in_specs = [pl.BlockSpec(memory_space=pltpu.SMEM),
                blk(0), blk(N_HEADS), blk(2 * N_HEADS), blk(3 * N_HEADS),
                pl.BlockSpec((None, 1, HEAD_DIM), lambda bi, h: (h, 0, 0))]
    args = [log_gamma, proj, proj, proj, proj, g_head.reshape(N_HEADS, 1, HEAD_DIM)]
    if rope_tabs is not None:
        in_specs += [pl.BlockSpec((t, HEAD_DIM), lambda bi, h: (0, 0))] * 2
        args += list(rope_tabs)
    if s0 is not None:
        in_specs.append(pl.BlockSpec((None, 2, None, HEAD_DIM, HEAD_DIM), lambda bi, h: (bi, 0, h, 0, 0)))
        args.append(s0)
    in_specs.append(pl.BlockSpec(memory_space=pl.ANY))
    args.append(y_mix)
    alias_idx = len(args) - 1
    out_specs = [pl.BlockSpec((t, HEAD_DIM), lambda bi, h: (rb0 + bi, h))]
    out_shape = [jax.ShapeDtypeStruct(y_mix.shape, y_mix.dtype)]
    if emit_state:
        out_specs.append(pl.BlockSpec((None, 2, None, HEAD_DIM, HEAD_DIM), lambda bi, h: (bi, 0, h, 0, 0)))
        out_shape.append(jax.ShapeDtypeStruct((b, 2, N_HEADS, HEAD_DIM, HEAD_DIM), F32))
    n_chunks = t // min(RET_CHUNK, t)
    outs = pl.pallas_call(
        functools.partial(_ret_kernel, t=t, rope=rope_tabs is not None, has_state=s0 is not None,
                          emit_state=emit_state),
        grid=(b, N_HEADS),
        in_specs=in_specs,
        out_specs=out_specs,
        out_shape=out_shape,
        scratch_shapes=[pltpu.VMEM((n_chunks, HEAD_DIM, HEAD_DIM), F32),
                        pltpu.VMEM((t, HEAD_DIM), F32),
                        pltpu.VMEM((t, HEAD_DIM), F32)],
        input_output_aliases={alias_idx: 0},
        compiler_params=_cp(2),
    )(*args)
    return outs[0], (outs[1] if emit_state else None)


def _hg_plan(c):
    t = np.arange(c)
    lower = (t[:, None] >= t[None, :]).astype(np.float32)
    stacks, masks = [], []
    for fwd in (True, False):
        tri = lower if fwd else lower.T
        end = c - 1 if fwd else 0
        blocks = [tri, tri[end:end + 1, :] - tri]
        level_masks = []
        w = 1
        while w < c:
            blk = t // (2 * w)
            late = (t % (2 * w)) >= w
            query = late if fwd else ~late
            ref = blk * 2 * w + (w - 1 if fwd else w)
            blocks.append(np.where(query, 1.0, -1.0)[:, None] * (tri - tri[ref, :]))
            level_masks.append(((blk[:, None] == blk[None, :]) & query[:, None] & ~query[None, :])
                               .astype(np.float32))
            w *= 2
        m = np.concatenate(blocks, axis=0)
        stacks.append(np.concatenate([m] * HG_SPLIT, axis=1))
        masks.append(np.stack(level_masks))
    return np.stack(stacks), np.stack(masks)


def _split3(x):
    hi = x.astype(BF16)
    r1 = x - hi.astype(F32)
    mid = r1.astype(BF16)
    lo = (r1 - mid.astype(F32)).astype(BF16)
    return hi, mid, lo


def _hg_kernel(*refs, t, has_state, emit_state):
    refs = list(refs)
    (q_ref, ff_ref, fb_ref, v_ref, gate_ref, lb_ref, gain_ref, m_ref, mask_ref) = [refs.pop(0) for _ in range(9)]
    if has_state:
        s0_ref = refs.pop(0)
    refs.pop(0)
    o_ref = refs.pop(0)
    if emit_state:
        st_ref = refs.pop(0)
    qa, k_fw, l_fw, k_bw, l_bw, o_fw, o_bw, st = refs

    c = min(HG_CHUNK, t)
    n_chunks = t // c
    n_levels = mask_ref.shape[1]
    hp = HG_HEADS_PER_STEP
    heads = [slice(hh * HEAD_DIM, (hh + 1) * HEAD_DIM) for hh in range(hp)]
    lbh = lb_ref[...]
    qa[...] = _silu(q_ref[...])

    def forget(z_ref, k_out, l_out):
        sg = jax.nn.sigmoid(z_ref[...])
        l_out[...] = jnp.log(jnp.maximum(lbh + (1.0 - lbh) * sg, F_MIN))
        k_out[...] = (1.0 - lbh) * (1.0 - sg)

    forget(ff_ref, k_fw, l_fw)
    forget(fb_ref, k_bw, l_bw)

    for d in range(2):
        for hh in range(hp):
            st[d, hh] = s0_ref[d, hh].T if has_state else jnp.zeros((HEAD_DIM, HEAD_DIM), F32)

    row = lax.broadcasted_iota(jnp.int32, (c, c), 0)
    col = lax.broadcasted_iota(jnp.int32, (c, c), 1)

    def rows_of(i):
        return pl.ds(pl.multiple_of(i * c, c), c)

    def direction(d, sl, own):
        q, k, vb = qa[sl, :], (k_fw, k_bw)[d][sl, :], v_ref[sl, :].astype(BF16)
        pieces = _split3((l_fw, l_bw)[d][sl, :])[:HG_SPLIT]
        e = jnp.exp(_dot(m_ref[d], jnp.concatenate(pieces, axis=0)))
        end = c - 1 if d == 0 else 0
        outs = []
        for hh, ln in enumerate(heads):
            qh, kh, vh = q[:, ln], k[:, ln], vb[:, ln]
            p = own[hh] if own is not None else jnp.zeros((c, c), F32)
            for j in range(n_levels):
                ej = e[(2 + j) * c:(3 + j) * c, ln]
                p = p + _dot_nt((qh * ej).astype(BF16), (kh * ej).astype(BF16)) * mask_ref[d, j]
            s = st[d, hh]
            outs.append(_dot(p.astype(BF16), vh) + _dot_nt((qh * e[0:c, ln]).astype(BF16), s.astype(BF16)))
            st[d, hh] = s * e[end:end + 1, ln] + _dot_tn(vh, (kh * e[c:2 * c, ln]).astype(BF16))
        return jnp.concatenate(outs, axis=1)

    def body(i, carry):
        sl = rows_of(i)
        k_both = (k_fw[sl, :] + k_bw[sl, :]).astype(BF16)
        q = qa[sl, :].astype(BF16)
        own = [jnp.where(row == col, _dot_nt(q[:, ln], k_both[:, ln]), 0.0) for ln in heads]
        o_fw[sl, :] = direction(0, sl, own)
        sl = rows_of(n_chunks - 1 - i)
        o_bw[sl, :] = direction(1, sl, None)
        return carry

    lax.fori_loop(0, n_chunks, body, 0, unroll=min(HG_UNROLL, n_chunks))
    gate = _silu(gate_ref[...])
    gain = gain_ref[...]
    for ln in heads:
        o = o_fw[:, ln] + o_bw[:, ln]
        ms = jnp.mean(o * o, axis=-1, keepdims=True)
        o_ref[:, ln] = (o * lax.rsqrt(ms + EPS) * gain[:, ln] * gate[:, ln]).astype(BF16)
    if emit_state:
        for d in range(2):
            for hh in range(hp):
                st_ref[d, hh] = st[d, hh].T


def _hgrn2(proj, y_mix, lb, g_head, row0, b, t, s0, emit_state):
    rb0 = row0 // t
    c = min(HG_CHUNK, t)
    hp = HG_HEADS_PER_STEP
    lanes = hp * HEAD_DIM
    m_np, mask_np = _hg_plan(c)
    m_stack = jnp.asarray(m_np, BF16)
    lvl_mask = jnp.asarray(mask_np, F32)
    hblocks = N_HEADS // hp
    blk = lambda col0: pl.BlockSpec((t, lanes), lambda bi, h: (rb0 + bi, col0 + h))
    per_head = pl.BlockSpec((None, 1, lanes), lambda bi, h: (h, 0, 0))
    state_spec = pl.BlockSpec((None, 2, hp, HEAD_DIM, HEAD_DIM), lambda bi, h: (bi, 0, h, 0, 0))
    in_specs = [blk(4 * hblocks), blk(5 * hblocks), blk(6 * hblocks), blk(7 * hblocks), blk(8 * hblocks),
                per_head, per_head,
                pl.BlockSpec(m_stack.shape, lambda bi, h: (0, 0, 0)),
                pl.BlockSpec(lvl_mask.shape, lambda bi, h: (0, 0, 0, 0))]
    args = [proj] * 5 + [lb.reshape(hblocks, 1, lanes), g_head.reshape(hblocks, 1, lanes),
                         m_stack, lvl_mask]
    if s0 is not None:
        in_specs.append(state_spec)
        args.append(s0)
    in_specs.append(pl.BlockSpec(memory_space=pl.ANY))
    args.append(y_mix)
    alias_idx = len(args) - 1
    out_specs = [pl.BlockSpec((t, lanes), lambda bi, h: (rb0 + bi, hblocks + h))]
    out_shape = [jax.ShapeDtypeStruct(y_mix.shape, y_mix.dtype)]
    if emit_state:
        out_specs.append(state_spec)
        out_shape.append(jax.ShapeDtypeStruct((b, 2, N_HEADS, HEAD_DIM, HEAD_DIM), F32))
    seq = pltpu.VMEM((t, lanes), F32)
    outs = pl.pallas_call(
        functools.partial(_hg_kernel, t=t, has_state=s0 is not None, emit_state=emit_state),
        grid=(b, hblocks),
        in_specs=in_specs,
        out_specs=out_specs,
        out_shape=out_shape,
        scratch_shapes=[seq] * 7 + [pltpu.VMEM((2, hp, HEAD_DIM, HEAD_DIM), F32)],
        input_output_aliases={alias_idx: 0},
        compiler_params=_cp(2),
    )(*args)
    return outs[0], (outs[1] if emit_state else None)


def _block_scan(a, b, reverse):
    row = lax.broadcasted_iota(jnp.int32, a.shape, 0)
    k = 1
    while k < SUBLANES:
        shift = (SUBLANES - k) if reverse else k
        valid = (row < SUBLANES - k) if reverse else (row >= k)
        a_prev = pltpu.roll(a, shift, 0)
        b_prev = pltpu.roll(b, shift, 0)
        b = jnp.where(valid, a * b_prev + b, b)
        a = jnp.where(valid, a * a_prev, a)
        k *= 2
    return a, b


def _odd_kernel(*refs, t, has_state, emit_state):
    refs = list(refs)
    (gi_ref, xb_ref, cw_ref, cb_ref, wa_ref, wx_ref, ba_ref, bx_ref, sp_ref) = [refs.pop(0) for _ in range(9)]
    if has_state:
        s0_ref = refs.pop(0)
    refs.pop(0)
    o_ref = refs.pop(0)
    if emit_state:
        st_ref = refs.pop(0)
    xpad, xc, a_f, b_f, a_b, b_b = refs

    cb = xb_ref.shape[1]
    pad = SUBLANES
    zeros = jnp.zeros((pad, cb), F32)
    xpad[pl.ds(0, pad), :] = zeros
    xpad[pl.ds(pad + t, pad), :] = zeros
    xpad[pl.ds(pad, t), :] = xb_ref[...]
    for n in range(cb // RG_BLOCK):
        cols = pl.ds(n * RG_BLOCK, RG_BLOCK)
        xc[:, cols] = (cb_ref[:, cols]
                       + xpad[pl.ds(pad - 1, t), cols] * cw_ref[0:1, cols]
                       + xpad[pl.ds(pad, t), cols] * cw_ref[1:2, cols]
                       + xpad[pl.ds(pad + 1, t), cols] * cw_ref[2:3, cols]
                       + xpad[pl.ds(pad + 2, t), cols] * cw_ref[3:4, cols])

    rc = min(256, t)

    def gate_body(i, carry):
        sl = pl.ds(pl.multiple_of(i * rc, rc), rc)
        for n in range(cb // RG_BLOCK):
            cols = pl.ds(n * RG_BLOCK, RG_BLOCK)
            x_blk = xc[sl, cols]
            x_bf = x_blk.astype(BF16)
            for d, (a_out, b_out) in enumerate(((a_f, b_f), (a_b, b_b))):
                r = _sigmoid_tanh(_dot(x_bf, wa_ref[d, n]) + ba_ref[d, :, cols])
                gate_i = _sigmoid_tanh(_dot(x_bf, wx_ref[d, n]) + bx_ref[d, :, cols])
                log_a = -RG_C * r * sp_ref[d, :, cols]
                a = jnp.exp(log_a)
                a_out[sl, cols] = a
                z = jnp.maximum(-jnp.tanh(log_a) * (1.0 + a * a), F_MIN)
                b_out[sl, cols] = (z * lax.rsqrt(z)) * (gate_i * x_blk)
        return carry

    lax.fori_loop(0, t // rc, gate_body, 0)

    n_blocks = t // SUBLANES

    def rows_of(j):
        return pl.ds(pl.multiple_of(j * SUBLANES, SUBLANES), SUBLANES)

    def scan_body(j, carry):
        h_prev, h_next = carry
        sl = rows_of(j)
        a, b = _block_scan(a_f[sl, :], b_f[sl, :], False)
        h = a * h_prev + b
        b_f[sl, :] = h
        h_prev = jnp.broadcast_to(h[SUBLANES - 1:SUBLANES, :], h.shape)
        sl = rows_of(n_blocks - 1 - j)
        a, b = _block_scan(a_b[sl, :], b_b[sl, :], True)
        h = a * h_next + b
        b_b[sl, :] = h
        return h_prev, jnp.broadcast_to(h[0:1, :], h.shape)

    h0_f = s0_ref[0] if has_state else jnp.zeros((1, cb), F32)
    h0_b = s0_ref[1] if has_state else jnp.zeros((1, cb), F32)
    last_f, first_b = lax.fori_loop(
        0, n_blocks, scan_body,
        (jnp.broadcast_to(h0_f, (SUBLANES, cb)), jnp.broadcast_to(h0_b, (SUBLANES, cb))), unroll=2)

    def out_body(i, carry):
        sl = pl.ds(pl.multiple_of(i * rc, rc), rc)
        gate = jax.nn.gelu(gi_ref[sl, :], approximate=True)
        o_ref[sl, :] = ((b_f[sl, :] + b_b[sl, :]) * gate).astype(BF16)
        return carry

    lax.fori_loop(0, t // rc, out_body, 0)
    if emit_state:
        st_ref[0] = last_f[0:1, :]
        st_ref[1] = first_b[0:1, :]


def _rglru(proj, y_mix, conv_w, conv_b, w_a, w_x, b_a, b_x, softplus_neg_lam, row0, b, t, s0, emit_state):
    d = D_MODEL
    cb = ODD_CB
    ncb = d // cb
    nrb = cb // RG_BLOCK
    rb0 = row0 // t
    vec = lambda rows: pl.BlockSpec((rows, cb), lambda bi, j: (0, j))
    vec2 = pl.BlockSpec((2, 1, cb), lambda bi, j: (0, 0, j))
    wspec = pl.BlockSpec((2, nrb, RG_BLOCK, RG_BLOCK), lambda bi, j: (0, j, 0, 0))
    in_specs = [pl.BlockSpec((t, cb), lambda bi, j: (rb0 + bi, j)),
                pl.BlockSpec((t, cb), lambda bi, j: (rb0 + bi, ncb + j)),
                vec(4), vec(1), wspec, wspec, vec2, vec2, vec2]
    args = [proj, proj, conv_w, conv_b.reshape(1, d), w_a, w_x,
            b_a.reshape(2, 1, d), b_x.reshape(2, 1, d), softplus_neg_lam.reshape(2, 1, d)]
    if s0 is not None:
        in_specs.append(pl.BlockSpec((None, 2, 1, cb), lambda bi, j: (bi, 0, 0, j)))
        args.append(s0.reshape(b, 2, 1, d))
    in_specs.append(pl.BlockSpec(memory_space=pl.ANY))
    args.append(y_mix)
    alias_idx = len(args) - 1
    out_specs = [pl.BlockSpec((t, cb), lambda bi, j: (rb0 + bi, j))]
    out_shape = [jax.ShapeDtypeStruct(y_mix.shape, y_mix.dtype)]
    if emit_state:
        out_specs.append(pl.BlockSpec((None, 2, 1, cb), lambda bi, j: (bi, 0, 0, j)))
        out_shape.append(jax.ShapeDtypeStruct((b, 2, 1, d), F32))
    seq = pltpu.VMEM((t, cb), F32)
    outs = pl.pallas_call(
        functools.partial(_odd_kernel, t=t, has_state=s0 is not None, emit_state=emit_state),
        grid=(b, ncb),
        in_specs=in_specs,
        out_specs=out_specs,
        out_shape=out_shape,
        scratch_shapes=[pltpu.VMEM((t + 2 * SUBLANES, cb), F32), seq, seq, seq, seq, seq],
        input_output_aliases={alias_idx: 0},
        compiler_params=_cp(2),
    )(*args)
    return outs[0], (outs[1].reshape(b, 2, d) if emit_state else None)


def _route(logits):
    lane = lax.broadcasted_iota(jnp.int32, logits.shape, 1)
    neg = jnp.float32(-jnp.inf)
    big = jnp.int32(LANES)

    def arg_max(vals):
        m = jnp.max(vals, axis=-1, keepdims=True)
        return m, jnp.min(jnp.where(vals == m, lane, big), axis=-1, keepdims=True)

    g_logits = jnp.where(lane < N_GROUPS, logits, neg)
    g_max, g_sel = arg_max(g_logits)
    p_grp = 1.0 / jnp.sum(jnp.exp(g_logits - g_max), axis=-1, keepdims=True)
    lo = N_GROUPS + EXPERTS_PER_GROUP * g_sel
    e_logits = jnp.where((lane >= lo) & (lane < lo + EXPERTS_PER_GROUP), logits, neg)
    v1, i1 = arg_max(e_logits)
    v2, i2 = arg_max(jnp.where(lane == i1, neg, e_logits))
    e2 = jnp.exp(v2 - v1)
    w1 = p_grp / (1.0 + e2)
    w2 = p_grp * e2 / (1.0 + e2)
    ids = jnp.where(lane == 0, i1 - N_GROUPS, jnp.where(lane == 1, i2 - N_GROUPS, 0))
    wts = jnp.where(lane == 0, w1, jnp.where(lane == 1, w2, 0.0))
    return ids, wts


def _outproj_kernel(y_ref, w_ref, x_ref, g1_ref, gn_ref, sh_ref, sc_ref, wr_ref, br_ref,
                    xo_ref, h_ref, ids_ref, wts_ref):
    x = x_ref[...] + g1_ref[...] * _dot(y_ref[...], w_ref[...])
    xo_ref[...] = x
    h = _norm_mod(x, gn_ref[...], sc_ref[...], sh_ref[...])
    h_ref[...] = h
    wr = wr_ref[...]
    r_hi, r_mid, r_lo = [_dot(piece, wr) for piece in _split3(h)]
    to_mid, to_lo = LANES - ROUTE_COLS, LANES - 2 * ROUTE_COLS
    small = r_lo + pltpu.roll(r_mid, to_mid, 1) + pltpu.roll(r_hi, to_lo, 1)
    logits = ((small + r_mid) + pltpu.roll(r_hi, to_mid, 1)) + r_hi + br_ref[...]
    ids, wts = _route(logits)
    ids_ref[...] = ids
    wts_ref[...] = wts


def _outproj_route(y_mix, w_out, x, mod, g_norm2, w_route, b_route, rows, layer):
    n, d = x.shape
    tm = rows.tile(512)
    row_tile = pl.BlockSpec((tm, d), lambda i: (i, 0))
    slab = pl.BlockSpec((tm, LANES), lambda i: (i, 0))
    return pl.pallas_call(
        _outproj_kernel,
        grid=(n // tm,),
        in_specs=[row_tile,
                  pl.BlockSpec((d, d), lambda i: (0, 0), pipeline_mode=pl.Buffered(1)),
                  row_tile,
                  _mod_spec(rows, layer, 2, tm),
                  pl.BlockSpec((None, 1, d), lambda i: (layer, 0, 0)),
                  _mod_spec(rows, layer, 3, tm),
                  _mod_spec(rows, layer, 4, tm),
                  pl.BlockSpec((d, LANES), lambda i: (0, 0)),
                  pl.BlockSpec((1, LANES), lambda i: (0, 0))],
        out_specs=[row_tile, row_tile, slab, slab],
        out_shape=[jax.ShapeDtypeStruct((n, d), F32), jax.ShapeDtypeStruct((n, d), F32),
                   jax.ShapeDtypeStruct((n, LANES), jnp.int32), jax.ShapeDtypeStruct((n, LANES), F32)],
        compiler_params=_cp(1),
    )(y_mix, w_out, x, mod, g_norm2.reshape(-1, 1, d), mod, mod, w_route, b_route)


def _row_copy(src_hbm, dst, sem, r, src_row):
    return pltpu.make_async_copy(src_hbm.at[pl.ds(src_row, 1)], dst.at[pl.ds(r, 1)], sem)


def _gather_start(idx_ref, n_rows, src_hbm, dst, sem):
    for r in range(n_rows):
        _row_copy(src_hbm, dst, sem, r, idx_ref[0, 0, r]).start(priority=r % 2)


def _gather_wait(n_rows, src_hbm, dst, sem):
    def body(r, carry):
        _row_copy(src_hbm, dst, sem, r, 0).wait()
        return carry

    lax.fori_loop(0, n_rows, body, 0, unroll=True)


def _moe_kernel(tile_e_ref, n_used_ref, cur_ref, nxt_ref, h_hbm, w1_ref, w3_ref, w2_ref, o_ref,
                xbuf, sem, w1_bf, w3_bf, w2_bf):
    i = pl.program_id(0)
    n_used = n_used_ref[0]
    slot = lax.rem(i, 2)
    tm = xbuf.shape[1]

    new_expert = (i == 0) | (tile_e_ref[i] != tile_e_ref[jnp.maximum(i - 1, 0)])

    @pl.when((i < n_used) & new_expert)
    def _():
        w1_bf[...] = w1_ref[...].astype(BF16)
        w3_bf[...] = w3_ref[...].astype(BF16)
        w2_bf[...] = w2_ref[...].astype(BF16)

    @pl.when(i == 0)
    def _():
        _gather_start(cur_ref, tm, h_hbm, xbuf.at[0], sem.at[0])

    @pl.when(i + 1 < n_used)
    def _():
        _gather_start(nxt_ref, tm, h_hbm, xbuf.at[1 - slot], sem.at[1 - slot])

    @pl.when(i < n_used)
    def _():
        _gather_wait(tm, h_hbm, xbuf.at[slot], sem.at[slot])
        x = xbuf[slot].astype(BF16)
        hid = _silu(_dot(x, w1_bf[...])) * _dot(x, w3_bf[...])
        o_ref[...] = _dot(hid.astype(BF16), w2_bf[...])

    @pl.when(i >= n_used)
    def _():
        o_ref[...] = jnp.zeros(o_ref.shape, F32)


def _moe_experts(h2, tile_expert, n_used, src_rows, w1, w3, w2, layer):
    n, d = h2.shape
    n_tiles = tile_expert.shape[0]
    tm = MOE_TM
    r = n_tiles * tm
    idx_spec = lambda step: pl.BlockSpec(
        (1, 1, tm), lambda i, te, nu: (jnp.minimum(i + step, n_tiles - 1), 0, 0), memory_space=pltpu.SMEM)
    w_spec = lambda rows_, cols_: pl.BlockSpec((None, None, rows_, cols_),
                                               lambda i, te, nu: (layer, te[i], 0, 0))
    grid_spec = pltpu.PrefetchScalarGridSpec(
        num_scalar_prefetch=2,
        grid=(n_tiles,),
        in_specs=[idx_spec(0), idx_spec(1),
                  pl.BlockSpec(memory_space=pl.ANY),
                  w_spec(d, D_EXPERT), w_spec(d, D_EXPERT), w_spec(D_EXPERT, d)],
        out_specs=pl.BlockSpec((tm, d), lambda i, te, nu: (i, 0)),
        scratch_shapes=[pltpu.VMEM((2, tm, d), F32), pltpu.SemaphoreType.DMA((2,)),
                        pltpu.VMEM((d, D_EXPERT), BF16), pltpu.VMEM((d, D_EXPERT), BF16),
                        pltpu.VMEM((D_EXPERT, d), BF16)],
    )
    src3 = src_rows.reshape(n_tiles, 1, tm)
    return pl.pallas_call(
        _moe_kernel,
        grid_spec=grid_spec,
        out_shape=jax.ShapeDtypeStruct((r, d), F32),
        compiler_params=_cp(1),
    )(tile_expert, n_used, src3, src3, h2, w1, w3, w2)


def _invert_kernel(dest_ref, src_ref):
    step = pl.program_id(0)
    chunk = dest_ref.shape[2]

    @pl.when(step == 0)
    def _():
        def zero(j, carry):
            src_ref[j] = 0
            return carry

        lax.fori_loop(0, src_ref.shape[0], zero, 0, unroll=8)

    base = step * chunk

    def body(j, carry):
        src_ref[dest_ref[0, 0, j]] = lax.shift_right_logical(base + j, 1)
        return carry

    lax.fori_loop(0, chunk, body, 0, unroll=8)


def _invert_plan(dest_flat, n_slots):
    n_assign = dest_flat.shape[0]
    chunk = 2048 if n_assign % 2048 == 0 else n_assign
    return pl.pallas_call(
        _invert_kernel,
        grid=(n_assign // chunk,),
        in_specs=[pl.BlockSpec((1, 1, chunk), lambda s: (s, 0, 0), memory_space=pltpu.SMEM)],
        out_specs=pl.BlockSpec(memory_space=pltpu.SMEM),
        out_shape=jax.ShapeDtypeStruct((n_slots,), jnp.int32),
        compiler_params=_cp(1),
    )(dest_flat.reshape(-1, 1, chunk))


def _plan_routing(ids):
    n = ids.shape[0]
    tm = MOE_TM
    n_tiles = (2 * n) // tm + N_EXPERTS
    r = n_tiles * tm
    e_flat = ids[:, :2].reshape(-1)
    onehot = (e_flat[:, None] == jnp.arange(N_EXPERTS, dtype=jnp.int32)[None, :]).astype(jnp.int32)
    csum = jnp.cumsum(onehot, axis=0)
    rank = jnp.sum(onehot * csum, axis=1) - 1
    counts = csum[-1]
    padded = ((counts + tm - 1) // tm) * tm
    ends = jnp.cumsum(padded)
    offs = ends - padded
    dest = jnp.sum(onehot * offs[None, :], axis=1) + rank
    tile_start = jnp.arange(n_tiles, dtype=jnp.int32) * tm
    tile_expert = jnp.minimum(jnp.sum((tile_start[:, None] >= ends[None, :]).astype(jnp.int32), axis=1),
                              N_EXPERTS - 1).astype(jnp.int32)
    src_rows = _invert_plan(dest.astype(jnp.int32), r)
    n_used = (ends[-1:] // tm).astype(jnp.int32)
    return tile_expert, n_used, src_rows, dest.reshape(n, 2)


def _combine_kernel(cur_ref, nxt_ref, x_ref, wts_ref, g2_ref, gn_ref, sh_ref, sc_ref, ys_hbm,
                    out_a, out_b, buf, sem, *, n_ctx_tiles):
    i = pl.program_id(0)
    slot = lax.rem(i, 2)
    tm = x_ref.shape[0]

    @pl.when(i == 0)
    def _():
        _gather_start(cur_ref, 2 * tm, ys_hbm, buf.at[0], sem.at[0])

    @pl.when(i + 1 < pl.num_programs(0))
    def _():
        _gather_start(nxt_ref, 2 * tm, ys_hbm, buf.at[1 - slot], sem.at[1 - slot])

    _gather_wait(2 * tm, ys_hbm, buf.at[slot], sem.at[slot])
    wts = wts_ref[...]
    moe = buf[slot, pl.ds(0, tm), :] * wts[:, 0:1] + buf[slot, pl.ds(tm, tm), :] * wts[:, 1:2]
    x = x_ref[...] + g2_ref[...] * moe
    if n_ctx_tiles is None:
        out_a[...] = x
        out_b[...] = _norm_mod(x, gn_ref[...], sc_ref[...], sh_ref[...]).astype(BF16)
    else:
        ms = jnp.mean(x * x, axis=-1, keepdims=True)
        y = x * lax.rsqrt(ms + EPS) * gn_ref[...]

        @pl.when(i < n_ctx_tiles)
        def _():
            out_a[...] = y

        @pl.when(i >= n_ctx_tiles)
        def _():
            out_b[...] = y


def _combine(ys, dest, wts, x, mod, g_next, rows, layer, final):
    n, d = x.shape
    tm = rows.tile(256)
    n_tiles = n // tm
    dest3 = dest.reshape(n_tiles, tm, 2).transpose(0, 2, 1).reshape(n_tiles, 1, 2 * tm)
    idx_spec = lambda step: pl.BlockSpec(
        (1, 1, 2 * tm), lambda i: (jnp.minimum(i + step, n_tiles - 1), 0, 0), memory_space=pltpu.SMEM)
    row_tile = pl.BlockSpec((tm, d), lambda i: (i, 0))
    nxt = layer if final else layer + 1
    gain_spec = (pl.BlockSpec((1, d), lambda i: (0, 0)) if final
                 else pl.BlockSpec((None, 1, d), lambda i: (nxt, 0, 0)))
    gain = g_next.reshape(1, d) if final else g_next.reshape(-1, 1, d)
    if final:
        n_ctx_tiles = rows.n_ctx // tm
        out_specs = [pl.BlockSpec((tm, d), lambda i: (jnp.minimum(i, n_ctx_tiles - 1), 0)),
                     pl.BlockSpec((tm, d), lambda i: (jnp.maximum(i - n_ctx_tiles, 0), 0))]
        out_shape = [jax.ShapeDtypeStruct((rows.n_ctx, d), F32),
                     jax.ShapeDtypeStruct((n - rows.n_ctx, d), F32)]
    else:
        n_ctx_tiles = None
        out_specs = [row_tile, row_tile]
        out_shape = [jax.ShapeDtypeStruct((n, d), F32), jax.ShapeDtypeStruct((n, d), BF16)]
    return pl.pallas_call(
        functools.partial(_combine_kernel, n_ctx_tiles=n_ctx_tiles),
        grid=(n_tiles,),
        in_specs=[idx_spec(0), idx_spec(1),
                  row_tile,
                  pl.BlockSpec((tm, LANES), lambda i: (i, 0)),
                  _mod_spec(rows, layer, 5, tm),
                  gain_spec,
                  _mod_spec(rows, nxt, 0, tm),
                  _mod_spec(rows, nxt, 1, tm),
                  pl.BlockSpec(memory_space=pl.ANY)],
        out_specs=out_specs,
        out_shape=out_shape,
        scratch_shapes=[pltpu.VMEM((2, 2 * tm, d), F32), pltpu.SemaphoreType.DMA((2,))],
        compiler_params=_cp(1),
    )(dest3, dest3, x, wts, mod, gain, mod, mod, ys)


def kernel(x_prompt, x_sample, state_ret, state_hgrn, state_rglru, c, c_ctx, w_ada, b_ada, g_norm1, g_norm2, w_even_in, w_even_out, ret_decay, hg_lb_logits, g_ret_head, g_hg_head, w_odd_in, conv_w, conv_b, w_a, b_a, w_x, b_x, rg_lambda, w_odd_out, w_group, b_group, w_router, b_router, w1, w3, w2, g_final):
    bp, tp, d = x_prompt.shape
    bs, ts, _ = x_sample.shape
    depth = w_ada.shape[0]
    rows = _Rows(bp, tp, bs, ts)
    assert bs + 1 <= COND_ROWS

    lb_sm = jax.nn.softmax(hg_lb_logits.astype(F32), axis=0)
    lb_all = jnp.cumsum(lb_sm, axis=0) - lb_sm[0:1]
    log_gamma = -jnp.exp(ret_decay.astype(F32))
    softplus_neg_lam = jax.nn.softplus(-rg_lambda.astype(F32))
    t_idx = jnp.arange(ts)
    freqs = ROPE_BASE ** (-jnp.arange(HEAD_DIM // 4, dtype=F32) / (HEAD_DIM // 4))
    ang = jnp.concatenate([(t_idx // GRID_W).astype(F32)[:, None] * freqs,
                           (t_idx % GRID_W).astype(F32)[:, None] * freqs], axis=-1)
    rope_tabs = (jnp.concatenate([jnp.cos(ang), jnp.cos(ang)], axis=-1),
                 jnp.concatenate([-jnp.sin(ang), jnp.sin(ang)], axis=-1))
    w_route = jnp.concatenate(
        list(_split3(jnp.concatenate([w_group, w_router], axis=-1).astype(F32)))
        + [jnp.zeros((depth, d, LANES - 3 * ROUTE_COLS), BF16)], axis=-1)
    b_route = jnp.concatenate(
        [b_group, b_router, jnp.zeros((depth, LANES - ROUTE_COLS), F32)], axis=-1)
    bf = lambda w: w.astype(BF16)

    cond = jnp.zeros((COND_ROWS, d), F32).at[:bs].set(c).at[bs].set(c_ctx)
    mod = _adaln_all(cond, w_ada, b_ada)

    x = jnp.concatenate([x_prompt.reshape(bp * tp, d), x_sample.reshape(bs * ts, d)], axis=0)
    h = _norm0(x, g_norm1, mod, rows, 0)
    new_ret, new_hg, new_rg = [], [], []
    y_mix = jnp.zeros((rows.n, d), BF16)
    for l in range(depth):
        if l % 2 == 0:
            e = l // 2
            proj = _in_proj(h, w_even_in, e)
            y_mix, sr = _retention(proj, y_mix, log_gamma[e], g_ret_head[e], 0, bp, tp, None, None, True)
            y_mix, _ = _retention(proj, y_mix, log_gamma[e], g_ret_head[e], rows.n_ctx, bs, ts, rope_tabs,
                                  state_ret[:, e], False)
            y_mix, sh = _hgrn2(proj, y_mix, lb_all[e], g_hg_head[e], 0, bp, tp, None, True)
            y_mix, _ = _hgrn2(proj, y_mix, lb_all[e], g_hg_head[e], rows.n_ctx, bs, ts, state_hgrn[:, e], False)
            new_ret.append(sr)
            new_hg.append(sh)
            w_out = bf(w_even_out[e])
        else:
            o = l // 2
            proj = _in_proj(h, w_odd_in, o)
            wa, wx = bf(w_a[o]), bf(w_x[o])
            y_mix, sg = _rglru(proj, y_mix, conv_w[o], conv_b[o], wa, wx, b_a[o], b_x[o], softplus_neg_lam[o],
                               0, bp, tp, None, True)
            y_mix, _ = _rglru(proj, y_mix, conv_w[o], conv_b[o], wa, wx, b_a[o], b_x[o], softplus_neg_lam[o],
                              rows.n_ctx, bs, ts, state_rglru[:, o], False)
            new_rg.append(sg)
            w_out = bf(w_odd_out[o])
        x, h2, ids, wts = _outproj_route(y_mix, w_out, x, mod, g_norm2, w_route[l], b_route[l:l + 1], rows, l)
        tile_expert, n_used, src_rows, dest = _plan_routing(ids)
        ys = _moe_experts(h2, tile_expert, n_used, src_rows, w1, w3, w2, l)
        final = l == depth - 1
        x, h = _combine(ys, dest, wts, x, mod, g_final if final else g_norm1, rows, l, final)

    return (x.reshape(bp, tp, d), h.reshape(bs, ts, d), jnp.stack(new_ret, axis=1),
            jnp.stack(new_hg, axis=1), jnp.stack(new_rg, axis=1))
```

```python
import functools

import numpy as np
import jax
import jax.numpy as jnp
from jax import lax
from jax.experimental import pallas as pl
from jax.experimental.pallas import tpu as pltpu

F32 = jnp.float32
BF16 = jnp.bfloat16
HIGHEST = lax.Precision.HIGHEST

D_MODEL = 2048
GRID_W = 64
HEAD_DIM = 128
N_HEADS = 8
MIX_HALF = N_HEADS * HEAD_DIM
EVEN_IN = 9 * MIX_HALF
ROPE_BASE = 10000.0
RG_BLOCK = 128
RG_C = 8.0
N_GROUPS = 4
EXPERTS_PER_GROUP = 4
N_EXPERTS = N_GROUPS * EXPERTS_PER_GROUP
D_EXPERT = 512
EPS = 1e-6
F_MIN = 1e-20
COND_ROWS = 16
LANES = 128
SUBLANES = 8

RET_CHUNK = 128
HG_CHUNK = 128
HG_UNROLL = 2
HG_HEADS_PER_STEP = 2
HG_SPLIT = 2
ODD_CB = 512
ROUTE_COLS = N_GROUPS + N_EXPERTS
MOE_TM = 256
VMEM_LIMIT = 52 * 1024 * 1024


def _cp(n_axes, vmem=VMEM_LIMIT):
    return pltpu.CompilerParams(dimension_semantics=("arbitrary",) * n_axes, vmem_limit_bytes=vmem)


def _silu(x):
    return x * jax.nn.sigmoid(x)


def _dot(a, b, **kw):
    return jnp.dot(a, b, preferred_element_type=F32, **kw)


def _dot_nt(a, b):
    return lax.dot_general(a, b, (((1,), (1,)), ((), ())), preferred_element_type=F32)


def _dot_tn(a, b):
    return lax.dot_general(a, b, (((0,), (0,)), ((), ())), preferred_element_type=F32)


def _norm_mod(x, gain, scale, shift):
    ms = jnp.mean(x * x, axis=-1, keepdims=True)
    return x * lax.rsqrt(ms + EPS) * gain * (1.0 + scale) + shift


def _ada_kernel(cond_ref, w_ref, b_ref, o_ref):
    s3 = jnp.concatenate(_split3(_silu(cond_ref[...])), axis=0)
    w_hi, w_mid, _ = _split3(w_ref[...])
    r_hi, r_mid = _dot(s3, w_hi), _dot(s3, w_mid)
    n = COND_ROWS
    o_ref[...] = ((r_mid[0:n] + r_mid[n:2 * n] + r_hi[2 * n:3 * n])
                  + r_hi[n:2 * n] + r_hi[0:n] + b_ref[...])


def _adaln_all(cond, w_ada, b_ada):
    depth, d, n6 = w_ada.shape
    tn = 1024
    out = pl.pallas_call(
        _ada_kernel,
        grid=(depth, n6 // tn),
        in_specs=[pl.BlockSpec((COND_ROWS, d), lambda l, j: (0, 0)),
                  pl.BlockSpec((None, d, tn), lambda l, j: (l, 0, j)),
                  pl.BlockSpec((None, 1, tn), lambda l, j: (l, 0, j))],
        out_specs=pl.BlockSpec((None, COND_ROWS, tn), lambda l, j: (l, 0, j)),
        out_shape=jax.ShapeDtypeStruct((depth, COND_ROWS, n6), F32),
        compiler_params=_cp(2),
    )(cond, w_ada, b_ada.reshape(depth, 1, n6))
    return out.reshape(depth, COND_ROWS, 6, 1, d)


class _Rows:
    def __init__(self, bp, tp, bs, ts):
        self.bp, self.tp, self.bs, self.ts = bp, tp, bs, ts
        self.n_ctx = bp * tp
        self.n = self.n_ctx + bs * ts
        self.ctx_row = bs

    def cond_row(self, i, tm):
        n_ctx_tiles = self.n_ctx // tm
        per_seq = self.ts // tm
        return jnp.where(i < n_ctx_tiles, self.ctx_row, (i - n_ctx_tiles) // per_seq)

    def tile(self, cap):
        tm = min(cap, self.ts)
        assert self.ts % tm == 0 and self.n_ctx % tm == 0
        return tm


def _mod_spec(rows, layer, which, tm):
    d = D_MODEL
    return pl.BlockSpec((None, None, None, 1, d),
                        lambda i, *_: (layer, rows.cond_row(i, tm), which, 0, 0))


def _norm0_kernel(x_ref, g_ref, sh_ref, sc_ref, h_ref):
    h_ref[...] = _norm_mod(x_ref[...], g_ref[...], sc_ref[...], sh_ref[...]).astype(BF16)


def _norm0(x, g_norm, mod, rows, layer):
    n, d = x.shape
    tm = rows.tile(512)
    return pl.pallas_call(
        _norm0_kernel,
        grid=(n // tm,),
        in_specs=[pl.BlockSpec((tm, d), lambda i: (i, 0)),
                  pl.BlockSpec((None, 1, d), lambda i: (layer, 0, 0)),
                  _mod_spec(rows, layer, 0, tm),
                  _mod_spec(rows, layer, 1, tm)],
        out_specs=pl.BlockSpec((tm, d), lambda i: (i, 0)),
        out_shape=jax.ShapeDtypeStruct((n, d), BF16),
        compiler_params=_cp(1),
    )(x, g_norm.reshape(-1, 1, d), mod, mod)


def _matmul_kernel(h_ref, w_ref, o_ref):
    o_ref[...] = _dot(h_ref[...], w_ref[...].astype(BF16))


def _in_proj(h, w, layer):
    n, k = h.shape
    _, _, nout = w.shape
    tm = 2048 if n % 2048 == 0 else 128
    tn = 512
    return pl.pallas_call(
        _matmul_kernel,
        grid=(n // tm, nout // tn),
        in_specs=[pl.BlockSpec((tm, k), lambda i, j: (i, 0)),
                  pl.BlockSpec((None, k, tn), lambda i, j: (layer, 0, j))],
        out_specs=pl.BlockSpec((tm, tn), lambda i, j: (i, j)),
        out_shape=jax.ShapeDtypeStruct((n, nout), F32),
        compiler_params=_cp(2),
    )(h, w)


def _ret_kernel(*refs, t, rope, has_state, emit_state):
    refs = list(refs)
    lg_ref = refs.pop(0)
    q_ref, k_ref, v_ref, g_ref, gain_ref = [refs.pop(0) for _ in range(5)]
    if rope:
        cos_ref, sin_ref = refs.pop(0), refs.pop(0)
    if has_state:
        s0_ref = refs.pop(0)
    refs.pop(0)
    o_ref = refs.pop(0)
    if emit_state:
        st_ref = refs.pop(0)
    sb_scr, qs, ks = refs

    c = min(RET_CHUNK, t)
    n_chunks = t // c
    head = pl.program_id(1)
    lgf = lg_ref[0, head]
    lgb = lg_ref[1, head]

    q = q_ref[...]
    k = k_ref[...]
    if rope:
        cs, sn = cos_ref[...], sin_ref[...]
        q = q * cs + pltpu.roll(q, HEAD_DIM // 2, 1) * sn
        k = k * cs + pltpu.roll(k, HEAD_DIM // 2, 1) * sn
    qs[...] = q
    ks[...] = k * (HEAD_DIM ** -0.5)

    pos_c = lax.broadcasted_iota(jnp.int32, (c, 1), 0).astype(F32)
    pos_r = lax.broadcasted_iota(jnp.int32, (1, c), 1).astype(F32)
    rel = pos_c - pos_r
    decay = (jnp.where(rel >= 0, jnp.exp(lgf * jnp.maximum(rel, 0.0)), 0.0)
             + jnp.where(rel <= 0, jnp.exp(lgb * jnp.maximum(-rel, 0.0)), 0.0))
    qdf = jnp.exp(lgf * (pos_c + 1.0))
    kdf = jnp.exp(lgf * (c - 1.0 - pos_c))
    qdb = jnp.exp(lgb * (c - pos_c))
    kdb = jnp.exp(lgb * pos_c)
    full = jnp.full((1, HEAD_DIM), float(c), F32)
    cdf = jnp.exp(lgf * full)
    cdb = jnp.exp(lgb * full)

    def rows_of(i):
        return pl.ds(pl.multiple_of(i * c, c), c)

    def bwd_body(ii, s):
        i = n_chunks - 1 - ii
        sb_scr[i] = s
        sl = rows_of(i)
        kv = _dot_tn((ks[sl, :] * kdb).astype(BF16), v_ref[sl, :].astype(BF16))
        return s * cdb + kv

    s0_b = s0_ref[1] if has_state else jnp.zeros((HEAD_DIM, HEAD_DIM), F32)
    s_b = lax.fori_loop(0, n_chunks, bwd_body, s0_b, unroll=True)

    gain = gain_ref[...]

    def fwd_body(i, s):
        sl = rows_of(i)
        qc, kc = qs[sl, :], ks[sl, :]
        vb = v_ref[sl, :].astype(BF16)
        scores = _dot_nt(qc.astype(BF16), kc.astype(BF16)) * decay
        o = (_dot(scores.astype(BF16), vb)
             + _dot((qc * qdf).astype(BF16), s.astype(BF16))
             + _dot((qc * qdb).astype(BF16), sb_scr[i].astype(BF16)))
        ms = jnp.mean(o * o, axis=-1, keepdims=True)
        o = o * lax.rsqrt(ms + EPS) * gain
        o_ref[sl, :] = (o * _silu(g_ref[sl, :])).astype(BF16)
        return s * cdf + _dot_tn((kc * kdf).astype(BF16), vb)

    s0_f = s0_ref[0] if has_state else jnp.zeros((HEAD_DIM, HEAD_DIM), F32)
    s_f = lax.fori_loop(0, n_chunks, fwd_body, s0_f, unroll=True)
    if emit_state:
        st_ref[0] = s_f
        st_ref[1] = s_b


def _retention(proj, y_mix, log_gamma, g_head, row0, b, t, rope_tabs, s0, emit_state):
    rb0 = row0 // t
    blk = lambda col0: pl.BlockSpec((t, HEAD_DIM), lambda bi, h: (rb0 + bi, col0 + h))
    in_specs = [pl.BlockSpec(memory_space=pltpu.SMEM),
                blk(0), blk(N_HEADS), blk(2 * N_HEADS), blk(3 * N_HEADS),
                pl.BlockSpec((None, 1, HEAD_DIM), lambda bi, h: (h, 0, 0))]
    args = [log_gamma, proj, proj, proj, proj, g_head.reshape(N_HEADS, 1, HEAD_DIM)]
    if rope_tabs is not None:
        in_specs += [pl.BlockSpec((t, HEAD_DIM), lambda bi, h: (0, 0))] * 2
        args += list(rope_tabs)
    if s0 is not None:
        in_specs.append(pl.BlockSpec((None, 2, None, HEAD_DIM, HEAD_DIM), lambda bi, h: (bi, 0, h, 0, 0)))
        args.append(s0)
    in_specs.append(pl.BlockSpec(memory_space=pl.ANY))
    args.append(y_mix)
    alias_idx = len(args) - 1
    out_specs = [pl.BlockSpec((t, HEAD_DIM), lambda bi, h: (rb0 + bi, h))]
    out_shape = [jax.ShapeDtypeStruct(y_mix.shape, y_mix.dtype)]
    if emit_state:
        out_specs.append(pl.BlockSpec((None, 2, None, HEAD_DIM, HEAD_DIM), lambda bi, h: (bi, 0, h, 0, 0)))
        out_shape.append(jax.ShapeDtypeStruct((b, 2, N_HEADS, HEAD_DIM, HEAD_DIM), F32))
    n_chunks = t // min(RET_CHUNK, t)
    outs = pl.pallas_call(
        functools.partial(_ret_kernel, t=t, rope=rope_tabs is not None, has_state=s0 is not None,
                          emit_state=emit_state),
        grid=(b, N_HEADS),
        in_specs=in_specs,
        out_specs=out_specs,
        out_shape=out_shape,
        scratch_shapes=[pltpu.VMEM((n_chunks, HEAD_DIM, HEAD_DIM), F32),
                        pltpu.VMEM((t, HEAD_DIM), F32),
                        pltpu.VMEM((t, HEAD_DIM), F32)],
        input_output_aliases={alias_idx: 0},
        compiler_params=_cp(2),
    )(*args)
    return outs[0], (outs[1] if emit_state else None)


def _hg_plan(c):
    t = np.arange(c)
    lower = (t[:, None] >= t[None, :]).astype(np.float32)
    stacks, masks = [], []
    for fwd in (True, False):
        tri = lower if fwd else lower.T
        end = c - 1 if fwd else 0
        blocks = [tri, tri[end:end + 1, :] - tri]
        level_masks = []
        w = 1
        while w < c:
            blk = t // (2 * w)
            late = (t % (2 * w)) >= w
            query = late if fwd else ~late
            ref = blk * 2 * w + (w - 1 if fwd else w)
            blocks.append(np.where(query, 1.0, -1.0)[:, None] * (tri - tri[ref, :]))
            level_masks.append(((blk[:, None] == blk[None, :]) & query[:, None] & ~query[None, :])
                               .astype(np.float32))
            w *= 2
        m = np.concatenate(blocks, axis=0)
        stacks.append(np.concatenate([m] * HG_SPLIT, axis=1))
        masks.append(np.stack(level_masks))
    return np.stack(stacks), np.stack(masks)


def _split3(x):
    hi = x.astype(BF16)
    r1 = x - hi.astype(F32)
    mid = r1.astype(BF16)
    lo = (r1 - mid.astype(F32)).astype(BF16)
    return hi, mid, lo


def _hg_kernel(*refs, t, has_state, emit_state):
    refs = list(refs)
    (q_ref, ff_ref, fb_ref, v_ref, gate_ref, lb_ref, gain_ref, m_ref, mask_ref) = [refs.pop(0) for _ in range(9)]
    if has_state:
        s0_ref = refs.pop(0)
    refs.pop(0)
    o_ref = refs.pop(0)
    if emit_state:
        st_ref = refs.pop(0)
    qa, k_fw, l_fw, k_bw, l_bw, o_fw, o_bw, st = refs

    c = min(HG_CHUNK, t)
    n_chunks = t // c
    n_levels = mask_ref.shape[1]
    hp = HG_HEADS_PER_STEP
    heads = [slice(hh * HEAD_DIM, (hh + 1) * HEAD_DIM) for hh in range(hp)]
    lbh = lb_ref[...]
    qa[...] = _silu(q_ref[...])

    def forget(z_ref, k_out, l_out):
        sg = jax.nn.sigmoid(z_ref[...])
        l_out[...] = jnp.log(jnp.maximum(lbh + (1.0 - lbh) * sg, F_MIN))
        k_out[...] = (1.0 - lbh) * (1.0 - sg)

    forget(ff_ref, k_fw, l_fw)
    forget(fb_ref, k_bw, l_bw)

    for d in range(2):
        for hh in range(hp):
            st[d, hh] = s0_ref[d, hh].T if has_state else jnp.zeros((HEAD_DIM, HEAD_DIM), F32)

    row = lax.broadcasted_iota(jnp.int32, (c, c), 0)
    col = lax.broadcasted_iota(jnp.int32, (c, c), 1)

    def rows_of(i):
        return pl.ds(pl.multiple_of(i * c, c), c)

    def direction(d, sl, own):
        q, k, vb = qa[sl, :], (k_fw, k_bw)[d][sl, :], v_ref[sl, :].astype(BF16)
        pieces = _split3((l_fw, l_bw)[d][sl, :])[:HG_SPLIT]
        e = jnp.exp(_dot(m_ref[d], jnp.concatenate(pieces, axis=0)))
        end = c - 1 if d == 0 else 0
        outs = []
        for hh, ln in enumerate(heads):
            qh, kh, vh = q[:, ln], k[:, ln], vb[:, ln]
            p = own[hh] if own is not None else jnp.zeros((c, c), F32)
            for j in range(n_levels):
                ej = e[(2 + j) * c:(3 + j) * c, ln]
                p = p + _dot_nt((qh * ej).astype(BF16), (kh * ej).astype(BF16)) * mask_ref[d, j]
            s = st[d, hh]
            outs.append(_dot(p.astype(BF16), vh) + _dot_nt((qh * e[0:c, ln]).astype(BF16), s.astype(BF16)))
            st[d, hh] = s * e[end:end + 1, ln] + _dot_tn(vh, (kh * e[c:2 * c, ln]).astype(BF16))
        return jnp.concatenate(outs, axis=1)

    def body(i, carry):
        sl = rows_of(i)
        k_both = (k_fw[sl, :] + k_bw[sl, :]).astype(BF16)
        q = qa[sl, :].astype(BF16)
        own = [jnp.where(row == col, _dot_nt(q[:, ln], k_both[:, ln]), 0.0) for ln in heads]
        o_fw[sl, :] = direction(0, sl, own)
        sl = rows_of(n_chunks - 1 - i)
        o_bw[sl, :] = direction(1, sl, None)
        return carry

    lax.fori_loop(0, n_chunks, body, 0, unroll=min(HG_UNROLL, n_chunks))
    gate = _silu(gate_ref[...])
    gain = gain_ref[...]
    for ln in heads:
        o = o_fw[:, ln] + o_bw[:, ln]
        ms = jnp.mean(o * o, axis=-1, keepdims=True)
        o_ref[:, ln] = (o * lax.rsqrt(ms + EPS) * gain[:, ln] * gate[:, ln]).astype(BF16)
    if emit_state:
        for d in range(2):
            for hh in range(hp):
                st_ref[d, hh] = st[d, hh].T


def _hgrn2(proj, y_mix, lb, g_head, row0, b, t, s0, emit_state):
    rb0 = row0 // t
    c = min(HG_CHUNK, t)
    hp = HG_HEADS_PER_STEP
    lanes = hp * HEAD_DIM
    m_np, mask_np = _hg_plan(c)
    m_stack = jnp.asarray(m_np, BF16)
    lvl_mask = jnp.asarray(mask_np, F32)
    hblocks = N_HEADS // hp
    blk = lambda col0: pl.BlockSpec((t, lanes), lambda bi, h: (rb0 + bi, col0 + h))
    per_head = pl.BlockSpec((None, 1, lanes), lambda bi, h: (h, 0, 0))
    state_spec = pl.BlockSpec((None, 2, hp, HEAD_DIM, HEAD_DIM), lambda bi, h: (bi, 0, h, 0, 0))
    in_specs = [blk(4 * hblocks), blk(5 * hblocks), blk(6 * hblocks), blk(7 * hblocks), blk(8 * hblocks),
                per_head, per_head,
                pl.BlockSpec(m_stack.shape, lambda bi, h: (0, 0, 0)),
                pl.BlockSpec(lvl_mask.shape, lambda bi, h: (0, 0, 0, 0))]
    args = [proj] * 5 + [lb.reshape(hblocks, 1, lanes), g_head.reshape(hblocks, 1, lanes),
                         m_stack, lvl_mask]
    if s0 is not None:
        in_specs.append(state_spec)
        args.append(s0)
    in_specs.append(pl.BlockSpec(memory_space=pl.ANY))
    args.append(y_mix)
    alias_idx = len(args) - 1
    out_specs = [pl.BlockSpec((t, lanes), lambda bi, h: (rb0 + bi, hblocks + h))]
    out_shape = [jax.ShapeDtypeStruct(y_mix.shape, y_mix.dtype)]
    if emit_state:
        out_specs.append(state_spec)
        out_shape.append(jax.ShapeDtypeStruct((b, 2, N_HEADS, HEAD_DIM, HEAD_DIM), F32))
    seq = pltpu.VMEM((t, lanes), F32)
    outs = pl.pallas_call(
        functools.partial(_hg_kernel, t=t, has_state=s0 is not None, emit_state=emit_state),
        grid=(b, hblocks),
        in_specs=in_specs,
        out_specs=out_specs,
        out_shape=out_shape,
        scratch_shapes=[seq] * 7 + [pltpu.VMEM((2, hp, HEAD_DIM, HEAD_DIM), F32)],
        input_output_aliases={alias_idx: 0},
        compiler_params=_cp(2),
    )(*args)
    return outs[0], (outs[1] if emit_state else None)


def _block_scan(a, b, reverse):
    row = lax.broadcasted_iota(jnp.int32, a.shape, 0)
    k = 1
    while k < SUBLANES:
        shift = (SUBLANES - k) if reverse else k
        valid = (row < SUBLANES - k) if reverse else (row >= k)
        a_prev = pltpu.roll(a, shift, 0)
        b_prev = pltpu.roll(b, shift, 0)
        b = jnp.where(valid, a * b_prev + b, b)
        a = jnp.where(valid, a * a_prev, a)
        k *= 2
    return a, b


def _odd_kernel(*refs, t, has_state, emit_state):
    refs = list(refs)
    (gi_ref, xb_ref, cw_ref, cb_ref, wa_ref, wx_ref, ba_ref, bx_ref, sp_ref) = [refs.pop(0) for _ in range(9)]
    if has_state:
        s0_ref = refs.pop(0)
    refs.pop(0)
    o_ref = refs.pop(0)
    if emit_state:
        st_ref = refs.pop(0)
    xpad, xc, a_f, b_f, a_b, b_b = refs

    cb = xb_ref.shape[1]
    pad = SUBLANES
    zeros = jnp.zeros((pad, cb), F32)
    xpad[pl.ds(0, pad), :] = zeros
    xpad[pl.ds(pad + t, pad), :] = zeros
    xpad[pl.ds(pad, t), :] = xb_ref[...]
    for n in range(cb // RG_BLOCK):
        cols = pl.ds(n * RG_BLOCK, RG_BLOCK)
        xc[:, cols] = (cb_ref[:, cols]
                       + xpad[pl.ds(pad - 1, t), cols] * cw_ref[0:1, cols]
                       + xpad[pl.ds(pad, t), cols] * cw_ref[1:2, cols]
                       + xpad[pl.ds(pad + 1, t), cols] * cw_ref[2:3, cols]
                       + xpad[pl.ds(pad + 2, t), cols] * cw_ref[3:4, cols])

    rc = min(256, t)

    def gate_body(i, carry):
        sl = pl.ds(pl.multiple_of(i * rc, rc), rc)
        for n in range(cb // RG_BLOCK):
            cols = pl.ds(n * RG_BLOCK, RG_BLOCK)
            x_blk = xc[sl, cols]
            x_bf = x_blk.astype(BF16)
            x_half = 0.5 * x_blk
            for d, (a_out, b_out) in enumerate(((a_f, b_f), (a_b, b_b))):
                t_r = jnp.tanh(_dot(x_bf, wa_ref[d, n]) + ba_ref[d, :, cols])
                t_i = jnp.tanh(_dot(x_bf, wx_ref[d, n]) + bx_ref[d, :, cols])
                log_a = sp_ref[d, :, cols] * (1.0 + t_r)
                a = jnp.exp(log_a)
                a_out[sl, cols] = a
                z = jnp.maximum(-jnp.tanh(log_a) * (1.0 + a * a), F_MIN)
                b_out[sl, cols] = (z * lax.rsqrt(z)) * (x_half + x_half * t_i)
        return carry

    lax.fori_loop(0, t // rc, gate_body, 0)

    n_blocks = t // SUBLANES

    def rows_of(j):
        return pl.ds(pl.multiple_of(j * SUBLANES, SUBLANES), SUBLANES)

    def scan_body(j, carry):
        h_prev, h_next = carry
        sl = rows_of(j)
        a, b = _block_scan(a_f[sl, :], b_f[sl, :], False)
        h = a * h_prev + b
        b_f[sl, :] = h
        h_prev = jnp.broadcast_to(h[SUBLANES - 1:SUBLANES, :], h.shape)
        sl = rows_of(n_blocks - 1 - j)
        a, b = _block_scan(a_b[sl, :], b_b[sl, :], True)
        h = a * h_next + b
        b_b[sl, :] = h
        return h_prev, jnp.broadcast_to(h[0:1, :], h.shape)

    h0_f = s0_ref[0] if has_state else jnp.zeros((1, cb), F32)
    h0_b = s0_ref[1] if has_state else jnp.zeros((1, cb), F32)
    last_f, first_b = lax.fori_loop(
        0, n_blocks, scan_body,
        (jnp.broadcast_to(h0_f, (SUBLANES, cb)), jnp.broadcast_to(h0_b, (SUBLANES, cb))), unroll=2)

    def out_body(i, carry):
        sl = pl.ds(pl.multiple_of(i * rc, rc), rc)
        gate = jax.nn.gelu(gi_ref[sl, :], approximate=True)
        o_ref[sl, :] = ((b_f[sl, :] + b_b[sl, :]) * gate).astype(BF16)
        return carry

    lax.fori_loop(0, t // rc, out_body, 0)
    if emit_state:
        st_ref[0] = last_f[0:1, :]
        st_ref[1] = first_b[0:1, :]


def _rglru(proj, y_mix, conv_w, conv_b, w_a, w_x, b_a, b_x, softplus_neg_lam, row0, b, t, s0, emit_state):
    d = D_MODEL
    cb = ODD_CB
    ncb = d // cb
    nrb = cb // RG_BLOCK
    rb0 = row0 // t
    vec = lambda rows: pl.BlockSpec((rows, cb), lambda bi, j: (0, j))
    vec2 = pl.BlockSpec((2, 1, cb), lambda bi, j: (0, 0, j))
    wspec = pl.BlockSpec((2, nrb, RG_BLOCK, RG_BLOCK), lambda bi, j: (0, j, 0, 0))
    in_specs = [pl.BlockSpec((t, cb), lambda bi, j: (rb0 + bi, j)),
                pl.BlockSpec((t, cb), lambda bi, j: (rb0 + bi, ncb + j)),
                vec(4), vec(1), wspec, wspec, vec2, vec2, vec2]
    args = [proj, proj, conv_w, conv_b.reshape(1, d), w_a, w_x,
            b_a.reshape(2, 1, d), b_x.reshape(2, 1, d), softplus_neg_lam.reshape(2, 1, d)]
    if s0 is not None:
        in_specs.append(pl.BlockSpec((None, 2, 1, cb), lambda bi, j: (bi, 0, 0, j)))
        args.append(s0.reshape(b, 2, 1, d))
    in_specs.append(pl.BlockSpec(memory_space=pl.ANY))
    args.append(y_mix)
    alias_idx = len(args) - 1
    out_specs = [pl.BlockSpec((t, cb), lambda bi, j: (rb0 + bi, j))]
    out_shape = [jax.ShapeDtypeStruct(y_mix.shape, y_mix.dtype)]
    if emit_state:
        out_specs.append(pl.BlockSpec((None, 2, 1, cb), lambda bi, j: (bi, 0, 0, j)))
        out_shape.append(jax.ShapeDtypeStruct((b, 2, 1, d), F32))
    seq = pltpu.VMEM((t, cb), F32)
    outs = pl.pallas_call(
        functools.partial(_odd_kernel, t=t, has_state=s0 is not None, emit_state=emit_state),
        grid=(b, ncb),
        in_specs=in_specs,
        out_specs=out_specs,
        out_shape=out_shape,
        scratch_shapes=[pltpu.VMEM((t + 2 * SUBLANES, cb), F32), seq, seq, seq, seq, seq],
        input_output_aliases={alias_idx: 0},
        compiler_params=_cp(2),
    )(*args)
    return outs[0], (outs[1].reshape(b, 2, d) if emit_state else None)


def _route(logits):
    lane = lax.broadcasted_iota(jnp.int32, logits.shape, 1)
    neg = jnp.float32(-jnp.inf)
    big = jnp.int32(LANES)

    def arg_max(vals):
        m = jnp.max(vals, axis=-1, keepdims=True)
        return m, jnp.min(jnp.where(vals == m, lane, big), axis=-1, keepdims=True)

    g_logits = jnp.where(lane < N_GROUPS, logits, neg)
    g_max, g_sel = arg_max(g_logits)
    p_grp = 1.0 / jnp.sum(jnp.exp(g_logits - g_max), axis=-1, keepdims=True)
    lo = N_GROUPS + EXPERTS_PER_GROUP * g_sel
    e_logits = jnp.where((lane >= lo) & (lane < lo + EXPERTS_PER_GROUP), logits, neg)
    v1, i1 = arg_max(e_logits)
    v2, i2 = arg_max(jnp.where(lane == i1, neg, e_logits))
    e2 = jnp.exp(v2 - v1)
    w1 = p_grp / (1.0 + e2)
    w2 = p_grp * e2 / (1.0 + e2)
    ids = jnp.where(lane == 0, i1 - N_GROUPS, jnp.where(lane == 1, i2 - N_GROUPS, 0))
    wts = jnp.where(lane == 0, w1, jnp.where(lane == 1, w2, 0.0))
    return ids, wts


def _outproj_kernel(y_ref, w_ref, x_ref, g1_ref, gn_ref, sh_ref, sc_ref, wr_ref, br_ref,
                    xo_ref, h_ref, ids_ref, wts_ref):
    x = x_ref[...] + g1_ref[...] * _dot(y_ref[...], w_ref[...])
    xo_ref[...] = x
    h = _norm_mod(x, gn_ref[...], sc_ref[...], sh_ref[...])
    h_ref[...] = h
    wr = wr_ref[...]
    r_hi, r_mid, r_lo = [_dot(piece, wr) for piece in _split3(h)]
    to_mid, to_lo = LANES - ROUTE_COLS, LANES - 2 * ROUTE_COLS
    small = r_lo + pltpu.roll(r_mid, to_mid, 1) + pltpu.roll(r_hi, to_lo, 1)
    logits = ((small + r_mid) + pltpu.roll(r_hi, to_mid, 1)) + r_hi + br_ref[...]
    ids, wts = _route(logits)
    ids_ref[...] = ids
    wts_ref[...] = wts


def _outproj_route(y_mix, w_out, x, mod, g_norm2, w_route, b_route, rows, layer):
    n, d = x.shape
    tm = rows.tile(512)
    row_tile = pl.BlockSpec((tm, d), lambda i: (i, 0))
    slab = pl.BlockSpec((tm, LANES), lambda i: (i, 0))
    return pl.pallas_call(
        _outproj_kernel,
        grid=(n // tm,),
        in_specs=[row_tile,
                  pl.BlockSpec((d, d), lambda i: (0, 0), pipeline_mode=pl.Buffered(1)),
                  row_tile,
                  _mod_spec(rows, layer, 2, tm),
                  pl.BlockSpec((None, 1, d), lambda i: (layer, 0, 0)),
                  _mod_spec(rows, layer, 3, tm),
                  _mod_spec(rows, layer, 4, tm),
                  pl.BlockSpec((d, LANES), lambda i: (0, 0)),
                  pl.BlockSpec((1, LANES), lambda i: (0, 0))],
        out_specs=[row_tile, row_tile, slab, slab],
        out_shape=[jax.ShapeDtypeStruct((n, d), F32), jax.ShapeDtypeStruct((n, d), F32),
                   jax.ShapeDtypeStruct((n, LANES), jnp.int32), jax.ShapeDtypeStruct((n, LANES), F32)],
        compiler_params=_cp(1),
    )(y_mix, w_out, x, mod, g_norm2.reshape(-1, 1, d), mod, mod, w_route, b_route)


def _row_copy(src_hbm, dst, sem, r, src_row):
    return pltpu.make_async_copy(src_hbm.at[pl.ds(src_row, 1)], dst.at[pl.ds(r, 1)], sem)


def _gather_start(idx_ref, n_rows, src_hbm, dst, sem):
    for r in range(n_rows):
        _row_copy(src_hbm, dst, sem, r, idx_ref[0, 0, r]).start(priority=r % 2)


def _gather_wait(n_rows, src_hbm, dst, sem):
    def body(r, carry):
        _row_copy(src_hbm, dst, sem, r, 0).wait()
        return carry

    lax.fori_loop(0, n_rows, body, 0, unroll=True)


def _moe_kernel(tile_e_ref, n_used_ref, cur_ref, nxt_ref, h_hbm, w1_ref, w3_ref, w2_ref, o_ref,
                xbuf, sem, w1_bf, w3_bf, w2_bf):
    i = pl.program_id(0)
    n_used = n_used_ref[0]
    slot = lax.rem(i, 2)
    tm = xbuf.shape[1]

    new_expert = (i == 0) | (tile_e_ref[i] != tile_e_ref[jnp.maximum(i - 1, 0)])

    @pl.when((i < n_used) & new_expert)
    def _():
        w1_bf[...] = w1_ref[...].astype(BF16)
        w3_bf[...] = w3_ref[...].astype(BF16)
        w2_bf[...] = w2_ref[...].astype(BF16)

    @pl.when(i == 0)
    def _():
        _gather_start(cur_ref, tm, h_hbm, xbuf.at[0], sem.at[0])

    @pl.when(i + 1 < n_used)
    def _():
        _gather_start(nxt_ref, tm, h_hbm, xbuf.at[1 - slot], sem.at[1 - slot])

    @pl.when(i < n_used)
    def _():
        _gather_wait(tm, h_hbm, xbuf.at[slot], sem.at[slot])
        x = xbuf[slot].astype(BF16)
        hid = _silu(_dot(x, w1_bf[...])) * _dot(x, w3_bf[...])
        o_ref[...] = _dot(hid.astype(BF16), w2_bf[...])

    @pl.when(i >= n_used)
    def _():
        o_ref[...] = jnp.zeros(o_ref.shape, F32)


def _moe_experts(h2, tile_expert, n_used, src_rows, w1, w3, w2, layer):
    n, d = h2.shape
    n_tiles = tile_expert.shape[0]
    tm = MOE_TM
    r = n_tiles * tm
    idx_spec = lambda step: pl.BlockSpec(
        (1, 1, tm), lambda i, te, nu: (jnp.minimum(i + step, n_tiles - 1), 0, 0), memory_space=pltpu.SMEM)
    w_spec = lambda rows_, cols_: pl.BlockSpec((None, None, rows_, cols_),
                                               lambda i, te, nu: (layer, te[i], 0, 0))
    grid_spec = pltpu.PrefetchScalarGridSpec(
        num_scalar_prefetch=2,
        grid=(n_tiles,),
        in_specs=[idx_spec(0), idx_spec(1),
                  pl.BlockSpec(memory_space=pl.ANY),
                  w_spec(d, D_EXPERT), w_spec(d, D_EXPERT), w_spec(D_EXPERT, d)],
        out_specs=pl.BlockSpec((tm, d), lambda i, te, nu: (i, 0)),
        scratch_shapes=[pltpu.VMEM((2, tm, d), F32), pltpu.SemaphoreType.DMA((2,)),
                        pltpu.VMEM((d, D_EXPERT), BF16), pltpu.VMEM((d, D_EXPERT), BF16),
                        pltpu.VMEM((D_EXPERT, d), BF16)],
    )
    src3 = src_rows.reshape(n_tiles, 1, tm)
    return pl.pallas_call(
        _moe_kernel,
        grid_spec=grid_spec,
        out_shape=jax.ShapeDtypeStruct((r, d), F32),
        compiler_params=_cp(1),
    )(tile_expert, n_used, src3, src3, h2, w1, w3, w2)


def _invert_kernel(dest_ref, zeros_hbm, src_ref):
    step = pl.program_id(0)
    chunk = dest_ref.shape[2]

    @pl.when(step == 0)
    def _():
        pltpu.sync_copy(zeros_hbm, src_ref)

    base = step * chunk

    def body(j, carry):
        tok = lax.shift_right_logical(base, 1) + j
        src_ref[dest_ref[0, 0, 2 * j]] = tok
        src_ref[dest_ref[0, 0, 2 * j + 1]] = tok
        return carry

    lax.fori_loop(0, chunk // 2, body, 0, unroll=8)


def _invert_plan(dest_flat, n_slots):
    n_assign = dest_flat.shape[0]
    chunk = 2048 if n_assign % 2048 == 0 else n_assign
    assert chunk % 2 == 0
    return pl.pallas_call(
        _invert_kernel,
        grid=(n_assign // chunk,),
        in_specs=[pl.BlockSpec((1, 1, chunk), lambda s: (s, 0, 0), memory_space=pltpu.SMEM),
                  pl.BlockSpec(memory_space=pl.ANY)],
        out_specs=pl.BlockSpec(memory_space=pltpu.SMEM),
        out_shape=jax.ShapeDtypeStruct((n_slots,), jnp.int32),
        compiler_params=_cp(1),
    )(dest_flat.reshape(-1, 1, chunk), jnp.zeros((n_slots,), jnp.int32))


def _plan_routing(ids):
    n = ids.shape[0]
    tm = MOE_TM
    n_tiles = (2 * n) // tm + N_EXPERTS
    r = n_tiles * tm
    e_flat = ids[:, :2].reshape(-1)
    onehot = (e_flat[:, None] == jnp.arange(N_EXPERTS, dtype=jnp.int32)[None, :]).astype(jnp.int32)
    csum = jnp.cumsum(onehot, axis=0)
    rank = jnp.sum(onehot * csum, axis=1) - 1
    counts = csum[-1]
    padded = ((counts + tm - 1) // tm) * tm
    ends = jnp.cumsum(padded)
    offs = ends - padded
    dest = jnp.sum(onehot * offs[None, :], axis=1) + rank
    tile_start = jnp.arange(n_tiles, dtype=jnp.int32) * tm
    tile_expert = jnp.minimum(jnp.sum((tile_start[:, None] >= ends[None, :]).astype(jnp.int32), axis=1),
                              N_EXPERTS - 1).astype(jnp.int32)
    src_rows = _invert_plan(dest.astype(jnp.int32), r)
    n_used = (ends[-1:] // tm).astype(jnp.int32)
    return tile_expert, n_used, src_rows, dest.reshape(n, 2)


def _combine_kernel(cur_ref, nxt_ref, x_ref, wts_ref, g2_ref, gn_ref, sh_ref, sc_ref, ys_hbm,
                    out_a, out_b, buf, sem, *, n_ctx_tiles):
    i = pl.program_id(0)
    slot = lax.rem(i, 2)
    tm = x_ref.shape[0]

    @pl.when(i == 0)
    def _():
        _gather_start(cur_ref, 2 * tm, ys_hbm, buf.at[0], sem.at[0])

    @pl.when(i + 1 < pl.num_programs(0))
    def _():
        _gather_start(nxt_ref, 2 * tm, ys_hbm, buf.at[1 - slot], sem.at[1 - slot])

    _gather_wait(2 * tm, ys_hbm, buf.at[slot], sem.at[slot])
    wts = wts_ref[...]
    moe = buf[slot, pl.ds(0, tm), :] * wts[:, 0:1] + buf[slot, pl.ds(tm, tm), :] * wts[:, 1:2]
    x = x_ref[...] + g2_ref[...] * moe
    if n_ctx_tiles is None:
        out_a[...] = x
        out_b[...] = _norm_mod(x, gn_ref[...], sc_ref[...], sh_ref[...]).astype(BF16)
    else:
        ms = jnp.mean(x * x, axis=-1, keepdims=True)
        y = x * lax.rsqrt(ms + EPS) * gn_ref[...]

        @pl.when(i < n_ctx_tiles)
        def _():
            out_a[...] = y

        @pl.when(i >= n_ctx_tiles)
        def _():
            out_b[...] = y


def _combine(ys, dest, wts, x, mod, g_next, rows, layer, final):
    n, d = x.shape
    tm = rows.tile(256)
    n_tiles = n // tm
    dest3 = dest.reshape(n_tiles, tm, 2).transpose(0, 2, 1).reshape(n_tiles, 1, 2 * tm)
    idx_spec = lambda step: pl.BlockSpec(
        (1, 1, 2 * tm), lambda i: (jnp.minimum(i + step, n_tiles - 1), 0, 0), memory_space=pltpu.SMEM)
    row_tile = pl.BlockSpec((tm, d), lambda i: (i, 0))
    nxt = layer if final else layer + 1
    gain_spec = (pl.BlockSpec((1, d), lambda i: (0, 0)) if final
                 else pl.BlockSpec((None, 1, d), lambda i: (nxt, 0, 0)))
    gain = g_next.reshape(1, d) if final else g_next.reshape(-1, 1, d)
    if final:
        n_ctx_tiles = rows.n_ctx // tm
        out_specs = [pl.BlockSpec((tm, d), lambda i: (jnp.minimum(i, n_ctx_tiles - 1), 0)),
                     pl.BlockSpec((tm, d), lambda i: (jnp.maximum(i - n_ctx_tiles, 0), 0))]
        out_shape = [jax.ShapeDtypeStruct((rows.n_ctx, d), F32),
                     jax.ShapeDtypeStruct((n - rows.n_ctx, d), F32)]
    else:
        n_ctx_tiles = None
        out_specs = [row_tile, row_tile]
        out_shape = [jax.ShapeDtypeStruct((n, d), F32), jax.ShapeDtypeStruct((n, d), BF16)]
    return pl.pallas_call(
        functools.partial(_combine_kernel, n_ctx_tiles=n_ctx_tiles),
        grid=(n_tiles,),
        in_specs=[idx_spec(0), idx_spec(1),
                  row_tile,
                  pl.BlockSpec((tm, LANES), lambda i: (i, 0)),
                  _mod_spec(rows, layer, 5, tm),
                  gain_spec,
                  _mod_spec(rows, nxt, 0, tm),
                  _mod_spec(rows, nxt, 1, tm),
                  pl.BlockSpec(memory_space=pl.ANY)],
        out_specs=out_specs,
        out_shape=out_shape,
        scratch_shapes=[pltpu.VMEM((2, 2 * tm, d), F32), pltpu.SemaphoreType.DMA((2,))],
        compiler_params=_cp(1),
    )(dest3, dest3, x, wts, mod, gain, mod, mod, ys)


def kernel(x_prompt, x_sample, state_ret, state_hgrn, state_rglru, c, c_ctx, w_ada, b_ada, g_norm1, g_norm2, w_even_in, w_even_out, ret_decay, hg_lb_logits, g_ret_head, g_hg_head, w_odd_in, conv_w, conv_b, w_a, b_a, w_x, b_x, rg_lambda, w_odd_out, w_group, b_group, w_router, b_router, w1, w3, w2, g_final):
    bp, tp, d = x_prompt.shape
    bs, ts, _ = x_sample.shape
    depth = w_ada.shape[0]
    rows = _Rows(bp, tp, bs, ts)
    assert bs + 1 <= COND_ROWS

    lb_sm = jax.nn.softmax(hg_lb_logits.astype(F32), axis=0)
    lb_all = jnp.cumsum(lb_sm, axis=0) - lb_sm[0:1]
    log_gamma = -jnp.exp(ret_decay.astype(F32))
    softplus_neg_lam = jax.nn.softplus(-rg_lambda.astype(F32))
    t_idx = jnp.arange(ts)
    freqs = ROPE_BASE ** (-jnp.arange(HEAD_DIM // 4, dtype=F32) / (HEAD_DIM // 4))
    ang = jnp.concatenate([(t_idx // GRID_W).astype(F32)[:, None] * freqs,
                           (t_idx % GRID_W).astype(F32)[:, None] * freqs], axis=-1)
    rope_tabs = (jnp.concatenate([jnp.cos(ang), jnp.cos(ang)], axis=-1),
                 jnp.concatenate([-jnp.sin(ang), jnp.sin(ang)], axis=-1))
    w_route = jnp.concatenate(
        list(_split3(jnp.concatenate([w_group, w_router], axis=-1).astype(F32)))
        + [jnp.zeros((depth, d, LANES - 3 * ROUTE_COLS), BF16)], axis=-1)
    b_route = jnp.concatenate(
        [b_group, b_router, jnp.zeros((depth, LANES - ROUTE_COLS), F32)], axis=-1)
    bf = lambda w: w.astype(BF16)

    cond = jnp.zeros((COND_ROWS, d), F32).at[:bs].set(c).at[bs].set(c_ctx)
    mod = _adaln_all(cond, w_ada, b_ada)

    x = jnp.concatenate([x_prompt.reshape(bp * tp, d), x_sample.reshape(bs * ts, d)], axis=0)
    h = _norm0(x, g_norm1, mod, rows, 0)
    new_ret, new_hg, new_rg = [], [], []
    y_mix = jnp.zeros((rows.n, d), BF16)
    for l in range(depth):
        if l % 2 == 0:
            e = l // 2
            proj = _in_proj(h, w_even_in, e)
            y_mix, sr = _retention(proj, y_mix, log_gamma[e], g_ret_head[e], 0, bp, tp, None, None, True)
            y_mix, _ = _retention(proj, y_mix, log_gamma[e], g_ret_head[e], rows.n_ctx, bs, ts, rope_tabs,
                                  state_ret[:, e], False)
            y_mix, sh = _hgrn2(proj, y_mix, lb_all[e], g_hg_head[e], 0, bp, tp, None, True)
            y_mix, _ = _hgrn2(proj, y_mix, lb_all[e], g_hg_head[e], rows.n_ctx, bs, ts, state_hgrn[:, e], False)
            new_ret.append(sr)
            new_hg.append(sh)
            w_out = bf(w_even_out[e])
        else:
            o = l // 2
            proj = _in_proj(h, w_odd_in, o)
            gate_args = (conv_w[o], conv_b[o], bf(0.5 * w_a[o]), bf(0.5 * w_x[o]), 0.5 * b_a[o], 0.5 * b_x[o],
                         (-0.5 * RG_C) * softplus_neg_lam[o])
            y_mix, sg = _rglru(proj, y_mix, *gate_args, 0, bp, tp, None, True)
            y_mix, _ = _rglru(proj, y_mix, *gate_args, rows.n_ctx, bs, ts, state_rglru[:, o], False)
            new_rg.append(sg)
            w_out = bf(w_odd_out[o])
        x, h2, ids, wts = _outproj_route(y_mix, w_out, x, mod, g_norm2, w_route[l], b_route[l:l + 1], rows, l)
        tile_expert, n_used, src_rows, dest = _plan_routing(ids)
        ys = _moe_experts(h2, tile_expert, n_used, src_rows, w1, w3, w2, l)
        final = l == depth - 1
        x, h = _combine(ys, dest, wts, x, mod, g_final if final else g_norm1, rows, l, final)

    return (x.reshape(bp, tp, d), h.reshape(bs, ts, d), jnp.stack(new_ret, axis=1),
            jnp.stack(new_hg, axis=1), jnp.stack(new_rg, axis=1))
```

```python
import functools

import numpy as np
import jax
import jax.numpy as jnp
from jax import lax
from jax.experimental import pallas as pl
from jax.experimental.pallas import tpu as pltpu

F32 = jnp.float32
BF16 = jnp.bfloat16
HIGHEST = lax.Precision.HIGHEST

D_MODEL = 2048
GRID_W = 64
HEAD_DIM = 128
N_HEADS = 8
MIX_HALF = N_HEADS * HEAD_DIM
EVEN_IN = 9 * MIX_HALF
ROPE_BASE = 10000.0
RG_BLOCK = 128
RG_C = 8.0
N_GROUPS = 4
EXPERTS_PER_GROUP = 4
N_EXPERTS = N_GROUPS * EXPERTS_PER_GROUP
D_EXPERT = 512
EPS = 1e-6
F_MIN = 1e-20
COND_ROWS = 16
LANES = 128
SUBLANES = 8

RET_CHUNK = 128
RET_HEADS_PER_STEP = 2
HG_CHUNK = 128
HG_UNROLL = 2
HG_HEADS_PER_STEP = 2
HG_SPLIT = 2
ODD_CB = 512
ROUTE_COLS = N_GROUPS + N_EXPERTS
MOE_TM = 256
VMEM_LIMIT = 52 * 1024 * 1024


def _cp(n_axes, vmem=VMEM_LIMIT):
    return pltpu.CompilerParams(dimension_semantics=("arbitrary",) * n_axes, vmem_limit_bytes=vmem)


def _silu(x):
    x_half = 0.5 * x
    return x_half + x_half * jnp.tanh(x_half)


def _dot(a, b, **kw):
    return jnp.dot(a, b, preferred_element_type=F32, **kw)


def _dot_nt(a, b):
    return lax.dot_general(a, b, (((1,), (1,)), ((), ())), preferred_element_type=F32)


def _dot_tn(a, b):
    return lax.dot_general(a, b, (((0,), (0,)), ((), ())), preferred_element_type=F32)


def _norm_mod(x, gain, scale, shift):
    ms = jnp.mean(x * x, axis=-1, keepdims=True)
    return x * lax.rsqrt(ms + EPS) * gain * (1.0 + scale) + shift


def _ada_kernel(cond_ref, w_ref, b_ref, o_ref):
    s3 = jnp.concatenate(_split3(_silu(cond_ref[...])), axis=0)
    w_hi, w_mid, _ = _split3(w_ref[...])
    r_hi, r_mid = _dot(s3, w_hi), _dot(s3, w_mid)
    n = COND_ROWS
    o_ref[...] = ((r_mid[0:n] + r_mid[n:2 * n] + r_hi[2 * n:3 * n])
                  + r_hi[n:2 * n] + r_hi[0:n] + b_ref[...])


def _adaln_all(cond, w_ada, b_ada):
    depth, d, n6 = w_ada.shape
    tn = 1024
    out = pl.pallas_call(
        _ada_kernel,
        grid=(depth, n6 // tn),
        in_specs=[pl.BlockSpec((COND_ROWS, d), lambda l, j: (0, 0)),
                  pl.BlockSpec((None, d, tn), lambda l, j: (l, 0, j)),
                  pl.BlockSpec((None, 1, tn), lambda l, j: (l, 0, j))],
        out_specs=pl.BlockSpec((None, COND_ROWS, tn), lambda l, j: (l, 0, j)),
        out_shape=jax.ShapeDtypeStruct((depth, COND_ROWS, n6), F32),
        compiler_params=_cp(2),
    )(cond, w_ada, b_ada.reshape(depth, 1, n6))
    return out.reshape(depth, COND_ROWS, 6, 1, d)


class _Rows:
    def __init__(self, bp, tp, bs, ts):
        self.bp, self.tp, self.bs, self.ts = bp, tp, bs, ts
        self.n_ctx = bp * tp
        self.n = self.n_ctx + bs * ts
        self.ctx_row = bs

    def cond_row(self, i, tm):
        n_ctx_tiles = self.n_ctx // tm
        per_seq = self.ts // tm
        return jnp.where(i < n_ctx_tiles, self.ctx_row, (i - n_ctx_tiles) // per_seq)

    def tile(self, cap):
        tm = min(cap, self.ts)
        assert self.ts % tm == 0 and self.n_ctx % tm == 0
        return tm


def _mod_spec(rows, layer, which, tm):
    d = D_MODEL
    return pl.BlockSpec((None, None, None, 1, d),
                        lambda i, *_: (layer, rows.cond_row(i, tm), which, 0, 0))


def _norm0_kernel(x_ref, g_ref, sh_ref, sc_ref, h_ref):
    h_ref[...] = _norm_mod(x_ref[...], g_ref[...], sc_ref[...], sh_ref[...]).astype(BF16)


def _norm0(x, g_norm, mod, rows, layer):
    n, d = x.shape
    tm = rows.tile(512)
    return pl.pallas_call(
        _norm0_kernel,
        grid=(n // tm,),
        in_specs=[pl.BlockSpec((tm, d), lambda i: (i, 0)),
                  pl.BlockSpec((None, 1, d), lambda i: (layer, 0, 0)),
                  _mod_spec(rows, layer, 0, tm),
                  _mod_spec(rows, layer, 1, tm)],
        out_specs=pl.BlockSpec((tm, d), lambda i: (i, 0)),
        out_shape=jax.ShapeDtypeStruct((n, d), BF16),
        compiler_params=_cp(1),
    )(x, g_norm.reshape(-1, 1, d), mod, mod)


def _matmul_kernel(h_ref, w_ref, o_ref):
    o_ref[...] = _dot(h_ref[...], w_ref[...].astype(BF16))


def _in_proj(h, w, layer):
    n, k = h.shape
    _, _, nout = w.shape
    tm = 2048 if n % 2048 == 0 else 128
    tn = 512
    return pl.pallas_call(
        _matmul_kernel,
        grid=(n // tm, nout // tn),
        in_specs=[pl.BlockSpec((tm, k), lambda i, j: (i, 0)),
                  pl.BlockSpec((None, k, tn), lambda i, j: (layer, 0, j))],
        out_specs=pl.BlockSpec((tm, tn), lambda i, j: (i, j)),
        out_shape=jax.ShapeDtypeStruct((n, nout), F32),
        compiler_params=_cp(2),
    )(h, w)


def _ret_kernel(*refs, t, rope, has_state, emit_state):
    refs = list(refs)
    lg_ref = refs.pop(0)
    q_ref, k_ref, v_ref, g_ref, gain_ref = [refs.pop(0) for _ in range(5)]
    if rope:
        cos_ref, sin_ref = refs.pop(0), refs.pop(0)
    if has_state:
        s0_ref = refs.pop(0)
    refs.pop(0)
    o_ref = refs.pop(0)
    if emit_state:
        st_ref = refs.pop(0)
    sb_scr, qs, ks = refs

    c = min(RET_CHUNK, t)
    n_chunks = t // c
    hp = RET_HEADS_PER_STEP
    pos_c = lax.broadcasted_iota(jnp.int32, (c, 1), 0).astype(F32)
    pos_r = lax.broadcasted_iota(jnp.int32, (1, c), 1).astype(F32)
    rel = pos_c - pos_r
    full = jnp.full((1, HEAD_DIM), float(c), F32)

    def rows_of(i):
        return pl.ds(pl.multiple_of(i * c, c), c)

    for hh in range(hp):
        ln = pl.ds(hh * HEAD_DIM, HEAD_DIM)
        head = pl.program_id(1) * hp + hh
        lgf = lg_ref[0, head]
        lgb = lg_ref[1, head]

        q = q_ref[:, ln]
        k = k_ref[:, ln]
        if rope:
            cs, sn = cos_ref[...], sin_ref[...]
            q = q * cs + pltpu.roll(q, HEAD_DIM // 2, 1) * sn
            k = k * cs + pltpu.roll(k, HEAD_DIM // 2, 1) * sn
        qs[:, ln] = q
        ks[:, ln] = k * (HEAD_DIM ** -0.5)

        decay = (jnp.where(rel >= 0, jnp.exp(lgf * jnp.maximum(rel, 0.0)), 0.0)
                 + jnp.where(rel <= 0, jnp.exp(lgb * jnp.maximum(-rel, 0.0)), 0.0))
        qdf = jnp.exp(lgf * (pos_c + 1.0))
        kdf = jnp.exp(lgf * (c - 1.0 - pos_c))
        qdb = jnp.exp(lgb * (c - pos_c))
        kdb = jnp.exp(lgb * pos_c)
        cdf = jnp.exp(lgf * full)
        cdb = jnp.exp(lgb * full)

        def bwd_body(ii, s, ln=ln, hh=hh, kdb=kdb, cdb=cdb):
            i = n_chunks - 1 - ii
            sb_scr[hh * n_chunks + i] = s
            sl = rows_of(i)
            kv = _dot_tn((ks[sl, ln] * kdb).astype(BF16), v_ref[sl, ln].astype(BF16))
            return s * cdb + kv

        s0_b = s0_ref[1, hh] if has_state else jnp.zeros((HEAD_DIM, HEAD_DIM), F32)
        s_b = lax.fori_loop(0, n_chunks, bwd_body, s0_b, unroll=True)

        gain = gain_ref[:, ln]

        def fwd_body(i, s, ln=ln, hh=hh, decay=decay, qdf=qdf, qdb=qdb, kdf=kdf, cdf=cdf, gain=gain):
            sl = rows_of(i)
            qc, kc = qs[sl, ln], ks[sl, ln]
            vb = v_ref[sl, ln].astype(BF16)
            scores = _dot_nt(qc.astype(BF16), kc.astype(BF16)) * decay
            o = (_dot(scores.astype(BF16), vb)
                 + _dot((qc * qdf).astype(BF16), s.astype(BF16))
                 + _dot((qc * qdb).astype(BF16), sb_scr[hh * n_chunks + i].astype(BF16)))
            ms = jnp.mean(o * o, axis=-1, keepdims=True)
            o = o * lax.rsqrt(ms + EPS) * gain
            o_ref[sl, ln] = (o * _silu(g_ref[sl, ln])).astype(BF16)
            return s * cdf + _dot_tn((kc * kdf).astype(BF16), vb)

        s0_f = s0_ref[0, hh] if has_state else jnp.zeros((HEAD_DIM, HEAD_DIM), F32)
        s_f = lax.fori_loop(0, n_chunks, fwd_body, s0_f, unroll=True)
        if emit_state:
            st_ref[0, hh] = s_f
            st_ref[1, hh] = s_b


def _retention(proj, y_mix, log_gamma, g_head, row0, b, t, rope_tabs, s0, emit_state):
    rb0 = row0 // t
    hp = RET_HEADS_PER_STEP
    lanes = hp * HEAD_DIM
    hblocks = N_HEADS // hp
    blk = lambda col0: pl.BlockSpec((t, lanes), lambda bi, h: (rb0 + bi, col0 + h))
    state_spec = pl.BlockSpec((None, 2, hp, HEAD_DIM, HEAD_DIM), lambda bi, h: (bi, 0, h, 0, 0))
    in_specs = [pl.BlockSpec(memory_space=pltpu.SMEM),
                blk(0), blk(hblocks), blk(2 * hblocks), blk(3 * hblocks),
                pl.BlockSpec((None, 1, lanes), lambda bi, h: (h, 0, 0))]
    args = [log_gamma, proj, proj, proj, proj, g_head.reshape(hblocks, 1, lanes)]
    if rope_tabs is not None:
        in_specs += [pl.BlockSpec((t, HEAD_DIM), lambda bi, h: (0, 0))] * 2
        args += list(rope_tabs)
    if s0 is not None:
        in_specs.append(state_spec)
        args.append(s0)
    in_specs.append(pl.BlockSpec(memory_space=pl.ANY))
    args.append(y_mix)
    alias_idx = len(args) - 1
    out_specs = [pl.BlockSpec((t, lanes), lambda bi, h: (rb0 + bi, h))]
    out_shape = [jax.ShapeDtypeStruct(y_mix.shape, y_mix.dtype)]
    if emit_state:
        out_specs.append(state_spec)
        out_shape.append(jax.ShapeDtypeStruct((b, 2, N_HEADS, HEAD_DIM, HEAD_DIM), F32))
    n_chunks = t // min(RET_CHUNK, t)
    outs = pl.pallas_call(
        functools.partial(_ret_kernel, t=t, rope=rope_tabs is not None, has_state=s0 is not None,
                          emit_state=emit_state),
        grid=(b, hblocks),
        in_specs=in_specs,
        out_specs=out_specs,
        out_shape=out_shape,
        scratch_shapes=[pltpu.VMEM((hp * n_chunks, HEAD_DIM, HEAD_DIM), F32),
                        pltpu.VMEM((t, lanes), F32),
                        pltpu.VMEM((t, lanes), F32)],
        input_output_aliases={alias_idx: 0},
        compiler_params=_cp(2),
    )(*args)
    return outs[0], (outs[1] if emit_state else None)


def _hg_plan(c):
    t = np.arange(c)
    lower = (t[:, None] >= t[None, :]).astype(np.float32)
    stacks, masks = [], []
    for fwd in (True, False):
        tri = lower if fwd else lower.T
        end = c - 1 if fwd else 0
        blocks = [tri, tri[end:end + 1, :] - tri]
        level_masks = []
        w = 1
        while w < c:
            blk = t // (2 * w)
            late = (t % (2 * w)) >= w
            query = late if fwd else ~late
            ref = blk * 2 * w + (w - 1 if fwd else w)
            blocks.append(np.where(query, 1.0, -1.0)[:, None] * (tri - tri[ref, :]))
            level_masks.append(((blk[:, None] == blk[None, :]) & query[:, None] & ~query[None, :])
                               .astype(np.float32))
            w *= 2
        m = np.concatenate(blocks, axis=0)
        stacks.append(np.concatenate([m] * HG_SPLIT, axis=1))
        masks.append(np.stack(level_masks))
    return np.stack(stacks), np.stack(masks)


def _split3(x):
    hi = x.astype(BF16)
    r1 = x - hi.astype(F32)
    mid = r1.astype(BF16)
    lo = (r1 - mid.astype(F32)).astype(BF16)
    return hi, mid, lo


def _hg_kernel(*refs, t, has_state, emit_state):
    refs = list(refs)
    (q_ref, ff_ref, fb_ref, v_ref, gate_ref, lb_ref, gain_ref, m_ref, mask_ref) = [refs.pop(0) for _ in range(9)]
    if has_state:
        s0_ref = refs.pop(0)
    refs.pop(0)
    o_ref = refs.pop(0)
    if emit_state:
        st_ref = refs.pop(0)
    qa, k_fw, l_fw, k_bw, l_bw, o_fw, o_bw, st = refs

    c = min(HG_CHUNK, t)
    n_chunks = t // c
    n_levels = mask_ref.shape[1]
    hp = HG_HEADS_PER_STEP
    heads = [slice(hh * HEAD_DIM, (hh + 1) * HEAD_DIM) for hh in range(hp)]
    lbh = lb_ref[...]
    qa[...] = _silu(q_ref[...])

    def forget(z_ref, k_out, l_out):
        sg = jax.nn.sigmoid(z_ref[...])
        l_out[...] = jnp.log(jnp.maximum(lbh + (1.0 - lbh) * sg, F_MIN))
        k_out[...] = (1.0 - lbh) * (1.0 - sg)

    forget(ff_ref, k_fw, l_fw)
    forget(fb_ref, k_bw, l_bw)

    for d in range(2):
        for hh in range(hp):
            st[d, hh] = s0_ref[d, hh].T if has_state else jnp.zeros((HEAD_DIM, HEAD_DIM), F32)

    row = lax.broadcasted_iota(jnp.int32, (c, c), 0)
    col = lax.broadcasted_iota(jnp.int32, (c, c), 1)

    def rows_of(i):
        return pl.ds(pl.multiple_of(i * c, c), c)

    def direction(d, sl, own):
        q, k, vb = qa[sl, :], (k_fw, k_bw)[d][sl, :], v_ref[sl, :].astype(BF16)
        pieces = _split3((l_fw, l_bw)[d][sl, :])[:HG_SPLIT]
        e = jnp.exp(_dot(m_ref[d], jnp.concatenate(pieces, axis=0)))
        end = c - 1 if d == 0 else 0
        outs = []
        for hh, ln in enumerate(heads):
            qh, kh, vh = q[:, ln], k[:, ln], vb[:, ln]
            p = own[hh] if own is not None else jnp.zeros((c, c), F32)
            for j in range(n_levels):
                ej = e[(2 + j) * c:(3 + j) * c, ln]
                p = p + _dot_nt((qh * ej).astype(BF16), (kh * ej).astype(BF16)) * mask_ref[d, j]
            s = st[d, hh]
            outs.append(_dot(p.astype(BF16), vh) + _dot_nt((qh * e[0:c, ln]).astype(BF16), s.astype(BF16)))
            st[d, hh] = s * e[end:end + 1, ln] + _dot_tn(vh, (kh * e[c:2 * c, ln]).astype(BF16))
        return jnp.concatenate(outs, axis=1)

    def body(i, carry):
        sl = rows_of(i)
        k_both = (k_fw[sl, :] + k_bw[sl, :]).astype(BF16)
        q = qa[sl, :].astype(BF16)
        own = [jnp.where(row == col, _dot_nt(q[:, ln], k_both[:, ln]), 0.0) for ln in heads]
        o_fw[sl, :] = direction(0, sl, own)
        sl = rows_of(n_chunks - 1 - i)
        o_bw[sl, :] = direction(1, sl, None)
        return carry

    lax.fori_loop(0, n_chunks, body, 0, unroll=min(HG_UNROLL, n_chunks))
    gate = _silu(gate_ref[...])
    gain = gain_ref[...]
    for ln in heads:
        o = o_fw[:, ln] + o_bw[:, ln]
        ms = jnp.mean(o * o, axis=-1, keepdims=True)
        o_ref[:, ln] = (o * lax.rsqrt(ms + EPS) * gain[:, ln] * gate[:, ln]).astype(BF16)
    if emit_state:
        for d in range(2):
            for hh in range(hp):
                st_ref[d, hh] = st[d, hh].T


def _hgrn2(proj, y_mix, lb, g_head, row0, b, t, s0, emit_state):
    rb0 = row0 // t
    c = min(HG_CHUNK, t)
    hp = HG_HEADS_PER_STEP
    lanes = hp * HEAD_DIM
    m_np, mask_np = _hg_plan(c)
    m_stack = jnp.asarray(m_np, BF16)
    lvl_mask = jnp.asarray(mask_np, F32)
    hblocks = N_HEADS // hp
    blk = lambda col0: pl.BlockSpec((t, lanes), lambda bi, h: (rb0 + bi, col0 + h))
    per_head = pl.BlockSpec((None, 1, lanes), lambda bi, h: (h, 0, 0))
    state_spec = pl.BlockSpec((None, 2, hp, HEAD_DIM, HEAD_DIM), lambda bi, h: (bi, 0, h, 0, 0))
    in_specs = [blk(4 * hblocks), blk(5 * hblocks), blk(6 * hblocks), blk(7 * hblocks), blk(8 * hblocks),
                per_head, per_head,
                pl.BlockSpec(m_stack.shape, lambda bi, h: (0, 0, 0)),
                pl.BlockSpec(lvl_mask.shape, lambda bi, h: (0, 0, 0, 0))]
    args = [proj] * 5 + [lb.reshape(hblocks, 1, lanes), g_head.reshape(hblocks, 1, lanes),
                         m_stack, lvl_mask]
    if s0 is not None:
        in_specs.append(state_spec)
        args.append(s0)
    in_specs.append(pl.BlockSpec(memory_space=pl.ANY))
    args.append(y_mix)
    alias_idx = len(args) - 1
    out_specs = [pl.BlockSpec((t, lanes), lambda bi, h: (rb0 + bi, hblocks + h))]
    out_shape = [jax.ShapeDtypeStruct(y_mix.shape, y_mix.dtype)]
    if emit_state:
        out_specs.append(state_spec)
        out_shape.append(jax.ShapeDtypeStruct((b, 2, N_HEADS, HEAD_DIM, HEAD_DIM), F32))
    seq = pltpu.VMEM((t, lanes), F32)
    outs = pl.pallas_call(
        functools.partial(_hg_kernel, t=t, has_state=s0 is not None, emit_state=emit_state),
        grid=(b, hblocks),
        in_specs=in_specs,
        out_specs=out_specs,
        out_shape=out_shape,
        scratch_shapes=[seq] * 7 + [pltpu.VMEM((2, hp, HEAD_DIM, HEAD_DIM), F32)],
        input_output_aliases={alias_idx: 0},
        compiler_params=_cp(2),
    )(*args)
    return outs[0], (outs[1] if emit_state else None)


def _block_scan(a, b, reverse):
    row = lax.broadcasted_iota(jnp.int32, a.shape, 0)
    k = 1
    while k < SUBLANES:
        shift = (SUBLANES - k) if reverse else k
        valid = (row < SUBLANES - k) if reverse else (row >= k)
        a_prev = pltpu.roll(a, shift, 0)
        b_prev = pltpu.roll(b, shift, 0)
        b = jnp.where(valid, a * b_prev + b, b)
        a = jnp.where(valid, a * a_prev, a)
        k *= 2
    return a, b


def _odd_kernel(*refs, t, has_state, emit_state):
    refs = list(refs)
    (gi_ref, xb_ref, cw_ref, cb_ref, wa_ref, wx_ref, ba_ref, bx_ref, sp_ref) = [refs.pop(0) for _ in range(9)]
    if has_state:
        s0_ref = refs.pop(0)
    refs.pop(0)
    o_ref = refs.pop(0)
    if emit_state:
        st_ref = refs.pop(0)
    xpad, xc, a_f, b_f, a_b, b_b = refs

    cb = xb_ref.shape[1]
    pad = SUBLANES
    zeros = jnp.zeros((pad, cb), F32)
    xpad[pl.ds(0, pad), :] = zeros
    xpad[pl.ds(pad + t, pad), :] = zeros
    xpad[pl.ds(pad, t), :] = xb_ref[...]
    for n in range(cb // RG_BLOCK):
        cols = pl.ds(n * RG_BLOCK, RG_BLOCK)
        xc[:, cols] = (cb_ref[:, cols]
                       + xpad[pl.ds(pad - 1, t), cols] * cw_ref[0:1, cols]
                       + xpad[pl.ds(pad, t), cols] * cw_ref[1:2, cols]
                       + xpad[pl.ds(pad + 1, t), cols] * cw_ref[2:3, cols]
                       + xpad[pl.ds(pad + 2, t), cols] * cw_ref[3:4, cols])

    rc = min(256, t)

    def gate_body(i, carry):
        sl = pl.ds(pl.multiple_of(i * rc, rc), rc)
        for n in range(cb // RG_BLOCK):
            cols = pl.ds(n * RG_BLOCK, RG_BLOCK)
            x_blk = xc[sl, cols]
            x_bf = x_blk.astype(BF16)
            x_half = 0.5 * x_blk
            for d, (a_out, b_out) in enumerate(((a_f, b_f), (a_b, b_b))):
                t_r = jnp.tanh(_dot(x_bf, wa_ref[d, n]) + ba_ref[d, :, cols])
                t_i = jnp.tanh(_dot(x_bf, wx_ref[d, n]) + bx_ref[d, :, cols])
                log_a = sp_ref[d, :, cols] * (1.0 + t_r)
                a = jnp.exp(log_a)
                a_out[sl, cols] = a
                z = jnp.maximum(-jnp.tanh(log_a) * (1.0 + a * a), F_MIN)
                b_out[sl, cols] = (z * lax.rsqrt(z)) * (x_half + x_half * t_i)
        return carry

    lax.fori_loop(0, t // rc, gate_body, 0)

    n_blocks = t // SUBLANES

    def rows_of(j):
        return pl.ds(pl.multiple_of(j * SUBLANES, SUBLANES), SUBLANES)

    def scan_body(j, carry):
        h_prev, h_next = carry
        sl = rows_of(j)
        a, b = _block_scan(a_f[sl, :], b_f[sl, :], False)
        h = a * h_prev + b
        b_f[sl, :] = h
        h_prev = jnp.broadcast_to(h[SUBLANES - 1:SUBLANES, :], h.shape)
        sl = rows_of(n_blocks - 1 - j)
        a, b = _block_scan(a_b[sl, :], b_b[sl, :], True)
        h = a * h_next + b
        b_b[sl, :] = h
        return h_prev, jnp.broadcast_to(h[0:1, :], h.shape)

    h0_f = s0_ref[0] if has_state else jnp.zeros((1, cb), F32)
    h0_b = s0_ref[1] if has_state else jnp.zeros((1, cb), F32)
    last_f, first_b = lax.fori_loop(
        0, n_blocks, scan_body,
        (jnp.broadcast_to(h0_f, (SUBLANES, cb)), jnp.broadcast_to(h0_b, (SUBLANES, cb))), unroll=2)

    def out_body(i, carry):
        sl = pl.ds(pl.multiple_of(i * rc, rc), rc)
        gate = jax.nn.gelu(gi_ref[sl, :], approximate=True)
        o_ref[sl, :] = ((b_f[sl, :] + b_b[sl, :]) * gate).astype(BF16)
        return carry

    lax.fori_loop(0, t // rc, out_body, 0)
    if emit_state:
        st_ref[0] = last_f[0:1, :]
        st_ref[1] = first_b[0:1, :]


def _rglru(proj, y_mix, conv_w, conv_b, w_a, w_x, b_a, b_x, softplus_neg_lam, row0, b, t, s0, emit_state):
    d = D_MODEL
    cb = ODD_CB
    ncb = d // cb
    nrb = cb // RG_BLOCK
    rb0 = row0 // t
    vec = lambda rows: pl.BlockSpec((rows, cb), lambda bi, j: (0, j))
    vec2 = pl.BlockSpec((2, 1, cb), lambda bi, j: (0, 0, j))
    wspec = pl.BlockSpec((2, nrb, RG_BLOCK, RG_BLOCK), lambda bi, j: (0, j, 0, 0))
    in_specs = [pl.BlockSpec((t, cb), lambda bi, j: (rb0 + bi, j)),
                pl.BlockSpec((t, cb), lambda bi, j: (rb0 + bi, ncb + j)),
                vec(4), vec(1), wspec, wspec, vec2, vec2, vec2]
    args = [proj, proj, conv_w, conv_b.reshape(1, d), w_a, w_x,
            b_a.reshape(2, 1, d), b_x.reshape(2, 1, d), softplus_neg_lam.reshape(2, 1, d)]
    if s0 is not None:
        in_specs.append(pl.BlockSpec((None, 2, 1, cb), lambda bi, j: (bi, 0, 0, j)))
        args.append(s0.reshape(b, 2, 1, d))
    in_specs.append(pl.BlockSpec(memory_space=pl.ANY))
    args.append(y_mix)
    alias_idx = len(args) - 1
    out_specs = [pl.BlockSpec((t, cb), lambda bi, j: (rb0 + bi, j))]
    out_shape = [jax.ShapeDtypeStruct(y_mix.shape, y_mix.dtype)]
    if emit_state:
        out_specs.append(pl.BlockSpec((None, 2, 1, cb), lambda bi, j: (bi, 0, 0, j)))
        out_shape.append(jax.ShapeDtypeStruct((b, 2, 1, d), F32))
    seq = pltpu.VMEM((t, cb), F32)
    outs = pl.pallas_call(
        functools.partial(_odd_kernel, t=t, has_state=s0 is not None, emit_state=emit_state),
        grid=(b, ncb),
        in_specs=in_specs,
        out_specs=out_specs,
        out_shape=out_shape,
        scratch_shapes=[pltpu.VMEM((t + 2 * SUBLANES, cb), F32), seq, seq, seq, seq, seq],
        input_output_aliases={alias_idx: 0},
        compiler_params=_cp(2),
    )(*args)
    return outs[0], (outs[1].reshape(b, 2, d) if emit_state else None)


def _route(logits):
    lane = lax.broadcasted_iota(jnp.int32, logits.shape, 1)
    neg = jnp.float32(-jnp.inf)
    big = jnp.int32(LANES)

    def arg_max(vals):
        m = jnp.max(vals, axis=-1, keepdims=True)
        return m, jnp.min(jnp.where(vals == m, lane, big), axis=-1, keepdims=True)

    g_logits = jnp.where(lane < N_GROUPS, logits, neg)
    g_max, g_sel = arg_max(g_logits)
    p_grp = 1.0 / jnp.sum(jnp.exp(g_logits - g_max), axis=-1, keepdims=True)
    lo = N_GROUPS + EXPERTS_PER_GROUP * g_sel
    e_logits = jnp.where((lane >= lo) & (lane < lo + EXPERTS_PER_GROUP), logits, neg)
    v1, i1 = arg_max(e_logits)
    v2, i2 = arg_max(jnp.where(lane == i1, neg, e_logits))
    e2 = jnp.exp(v2 - v1)
    w1 = p_grp / (1.0 + e2)
    w2 = p_grp * e2 / (1.0 + e2)
    ids = jnp.where(lane == 0, i1 - N_GROUPS, jnp.where(lane == 1, i2 - N_GROUPS, 0))
    wts = jnp.where(lane == 0, w1, jnp.where(lane == 1, w2, 0.0))
    return ids, wts


def _outproj_kernel(y_ref, w_ref, x_ref, g1_ref, gn_ref, sh_ref, sc_ref, wr_ref, br_ref,
                    xo_ref, h_ref, ids_ref, wts_ref):
    x = x_ref[...] + g1_ref[...] * _dot(y_ref[...], w_ref[...])
    xo_ref[...] = x
    h = _norm_mod(x, gn_ref[...], sc_ref[...], sh_ref[...])
    h_ref[...] = h
    wr = wr_ref[...]
    r_hi, r_mid, r_lo = [_dot(piece, wr) for piece in _split3(h)]
    to_mid, to_lo = LANES - ROUTE_COLS, LANES - 2 * ROUTE_COLS
    small = r_lo + pltpu.roll(r_mid, to_mid, 1) + pltpu.roll(r_hi, to_lo, 1)
    logits = ((small + r_mid) + pltpu.roll(r_hi, to_mid, 1)) + r_hi + br_ref[...]
    ids, wts = _route(logits)
    ids_ref[...] = ids
    wts_ref[...] = wts


def _outproj_route(y_mix, w_out, x, mod, g_norm2, w_route, b_route, rows, layer):
    n, d = x.shape
    tm = rows.tile(512)
    row_tile = pl.BlockSpec((tm, d), lambda i: (i, 0))
    slab = pl.BlockSpec((tm, LANES), lambda i: (i, 0))
    return pl.pallas_call(
        _outproj_kernel,
        grid=(n // tm,),
        in_specs=[row_tile,
                  pl.BlockSpec((d, d), lambda i: (0, 0), pipeline_mode=pl.Buffered(1)),
                  row_tile,
                  _mod_spec(rows, layer, 2, tm),
                  pl.BlockSpec((None, 1, d), lambda i: (layer, 0, 0)),
                  _mod_spec(rows, layer, 3, tm),
                  _mod_spec(rows, layer, 4, tm),
                  pl.BlockSpec((d, LANES), lambda i: (0, 0)),
                  pl.BlockSpec((1, LANES), lambda i: (0, 0))],
        out_specs=[row_tile, row_tile, slab, slab],
        out_shape=[jax.ShapeDtypeStruct((n, d), F32), jax.ShapeDtypeStruct((n, d), F32),
                   jax.ShapeDtypeStruct((n, LANES), jnp.int32), jax.ShapeDtypeStruct((n, LANES), F32)],
        compiler_params=_cp(1),
    )(y_mix, w_out, x, mod, g_norm2.reshape(-1, 1, d), mod, mod, w_route, b_route)


def _row_copy(src_hbm, dst, sem, r, src_row):
    return pltpu.make_async_copy(src_hbm.at[pl.ds(src_row, 1)], dst.at[pl.ds(r, 1)], sem)


def _gather_start(idx_ref, n_rows, src_hbm, dst, sem):
    for r in range(n_rows):
        _row_copy(src_hbm, dst, sem, r, idx_ref[0, 0, r]).start(priority=r % 2)


def _gather_wait(n_rows, src_hbm, dst, sem):
    def body(r, carry):
        _row_copy(src_hbm, dst, sem, r, 0).wait()
        return carry

    lax.fori_loop(0, n_rows, body, 0, unroll=True)


def _moe_kernel(tile_e_ref, n_used_ref, cur_ref, nxt_ref, h_hbm, w1_ref, w3_ref, w2_ref, o_ref,
                xbuf, sem, w1_bf, w3_bf, w2_bf):
    i = pl.program_id(0)
    n_used = n_used_ref[0]
    slot = lax.rem(i, 2)
    tm = xbuf.shape[1]

    new_expert = (i == 0) | (tile_e_ref[i] != tile_e_ref[jnp.maximum(i - 1, 0)])

    @pl.when((i < n_used) & new_expert)
    def _():
        w1_bf[...] = w1_ref[...].astype(BF16)
        w3_bf[...] = w3_ref[...].astype(BF16)
        w2_bf[...] = w2_ref[...].astype(BF16)

    @pl.when(i == 0)
    def _():
        _gather_start(cur_ref, tm, h_hbm, xbuf.at[0], sem.at[0])

    @pl.when(i + 1 < n_used)
    def _():
        _gather_start(nxt_ref, tm, h_hbm, xbuf.at[1 - slot], sem.at[1 - slot])

    @pl.when(i < n_used)
    def _():
        _gather_wait(tm, h_hbm, xbuf.at[slot], sem.at[slot])
        x = xbuf[slot].astype(BF16)
        hid = _silu(_dot(x, w1_bf[...])) * _dot(x, w3_bf[...])
        o_ref[...] = _dot(hid.astype(BF16), w2_bf[...])

    @pl.when(i >= n_used)
    def _():
        o_ref[...] = jnp.zeros(o_ref.shape, F32)


def _moe_experts(h2, tile_expert, n_used, src_rows, w1, w3, w2, layer):
    n, d = h2.shape
    n_tiles = tile_expert.shape[0]
    tm = MOE_TM
    r = n_tiles * tm
    idx_spec = lambda step: pl.BlockSpec(
        (1, 1, tm), lambda i, te, nu: (jnp.minimum(i + step, n_tiles - 1), 0, 0), memory_space=pltpu.SMEM)
    w_spec = lambda rows_, cols_: pl.BlockSpec((None, None, rows_, cols_),
                                               lambda i, te, nu: (layer, te[i], 0, 0))
    grid_spec = pltpu.PrefetchScalarGridSpec(
        num_scalar_prefetch=2,
        grid=(n_tiles,),
        in_specs=[idx_spec(0), idx_spec(1),
                  pl.BlockSpec(memory_space=pl.ANY),
                  w_spec(d, D_EXPERT), w_spec(d, D_EXPERT), w_spec(D_EXPERT, d)],
        out_specs=pl.BlockSpec((tm, d), lambda i, te, nu: (i, 0)),
        scratch_shapes=[pltpu.VMEM((2, tm, d), F32), pltpu.SemaphoreType.DMA((2,)),
                        pltpu.VMEM((d, D_EXPERT), BF16), pltpu.VMEM((d, D_EXPERT), BF16),
                        pltpu.VMEM((D_EXPERT, d), BF16)],
    )
    src3 = src_rows.reshape(n_tiles, 1, tm)
    return pl.pallas_call(
        _moe_kernel,
        grid_spec=grid_spec,
        out_shape=jax.ShapeDtypeStruct((r, d), F32),
        compiler_params=_cp(1),
    )(tile_expert, n_used, src3, src3, h2, w1, w3, w2)


def _invert_kernel(dest_ref, zeros_hbm, src_ref):
    step = pl.program_id(0)
    chunk = dest_ref.shape[2]

    @pl.when(step == 0)
    def _():
        pltpu.sync_copy(zeros_hbm, src_ref)

    base = step * chunk

    def body(j, carry):
        tok = lax.shift_right_logical(base, 1) + j
        src_ref[dest_ref[0, 0, 2 * j]] = tok
        src_ref[dest_ref[0, 0, 2 * j + 1]] = tok
        return carry

    lax.fori_loop(0, chunk // 2, body, 0, unroll=8)


def _invert_plan(dest_flat, n_slots):
    n_assign = dest_flat.shape[0]
    chunk = 2048 if n_assign % 2048 == 0 else n_assign
    assert chunk % 2 == 0
    return pl.pallas_call(
        _invert_kernel,
        grid=(n_assign // chunk,),
        in_specs=[pl.BlockSpec((1, 1, chunk), lambda s: (s, 0, 0), memory_space=pltpu.SMEM),
                  pl.BlockSpec(memory_space=pl.ANY)],
        out_specs=pl.BlockSpec(memory_space=pltpu.SMEM),
        out_shape=jax.ShapeDtypeStruct((n_slots,), jnp.int32),
        compiler_params=_cp(1),
    )(dest_flat.reshape(-1, 1, chunk), jnp.zeros((n_slots,), jnp.int32))


def _plan_routing(ids):
    n = ids.shape[0]
    tm = MOE_TM
    n_tiles = (2 * n) // tm + N_EXPERTS
    r = n_tiles * tm
    e_flat = ids[:, :2].reshape(-1)
    onehot = (e_flat[:, None] == jnp.arange(N_EXPERTS, dtype=jnp.int32)[None, :]).astype(jnp.int32)
    csum = jnp.cumsum(onehot, axis=0)
    rank = jnp.sum(onehot * csum, axis=1) - 1
    counts = csum[-1]
    padded = ((counts + tm - 1) // tm) * tm
    ends = jnp.cumsum(padded)
    offs = ends - padded
    dest = jnp.sum(onehot * offs[None, :], axis=1) + rank
    tile_start = jnp.arange(n_tiles, dtype=jnp.int32) * tm
    tile_expert = jnp.minimum(jnp.sum((tile_start[:, None] >= ends[None, :]).astype(jnp.int32), axis=1),
                              N_EXPERTS - 1).astype(jnp.int32)
    src_rows = _invert_plan(dest.astype(jnp.int32), r)
    n_used = (ends[-1:] // tm).astype(jnp.int32)
    return tile_expert, n_used, src_rows, dest.reshape(n, 2)


def _combine_kernel(cur_ref, nxt_ref, x_ref, wts_ref, g2_ref, gn_ref, sh_ref, sc_ref, ys_hbm,
                    out_a, out_b, buf, sem, *, n_ctx_tiles):
    i = pl.program_id(0)
    slot = lax.rem(i, 2)
    tm = x_ref.shape[0]

    @pl.when(i == 0)
    def _():
        _gather_start(cur_ref, 2 * tm, ys_hbm, buf.at[0], sem.at[0])

    @pl.when(i + 1 < pl.num_programs(0))
    def _():
        _gather_start(nxt_ref, 2 * tm, ys_hbm, buf.at[1 - slot], sem.at[1 - slot])

    _gather_wait(2 * tm, ys_hbm, buf.at[slot], sem.at[slot])
    wts = wts_ref[...]
    moe = buf[slot, pl.ds(0, tm), :] * wts[:, 0:1] + buf[slot, pl.ds(tm, tm), :] * wts[:, 1:2]
    x = x_ref[...] + g2_ref[...] * moe
    if n_ctx_tiles is None:
        out_a[...] = x
        out_b[...] = _norm_mod(x, gn_ref[...], sc_ref[...], sh_ref[...]).astype(BF16)
    else:
        ms = jnp.mean(x * x, axis=-1, keepdims=True)
        y = x * lax.rsqrt(ms + EPS) * gn_ref[...]

        @pl.when(i < n_ctx_tiles)
        def _():
            out_a[...] = y

        @pl.when(i >= n_ctx_tiles)
        def _():
            out_b[...] = y


def _combine(ys, dest, wts, x, mod, g_next, rows, layer, final):
    n, d = x.shape
    tm = rows.tile(256)
    n_tiles = n // tm
    dest3 = dest.reshape(n_tiles, tm, 2).transpose(0, 2, 1).reshape(n_tiles, 1, 2 * tm)
    idx_spec = lambda step: pl.BlockSpec(
        (1, 1, 2 * tm), lambda i: (jnp.minimum(i + step, n_tiles - 1), 0, 0), memory_space=pltpu.SMEM)
    row_tile = pl.BlockSpec((tm, d), lambda i: (i, 0))
    nxt = layer if final else layer + 1
    gain_spec = (pl.BlockSpec((1, d), lambda i: (0, 0)) if final
                 else pl.BlockSpec((None, 1, d), lambda i: (nxt, 0, 0)))
    gain = g_next.reshape(1, d) if final else g_next.reshape(-1, 1, d)
    if final:
        n_ctx_tiles = rows.n_ctx // tm
        out_specs = [pl.BlockSpec((tm, d), lambda i: (jnp.minimum(i, n_ctx_tiles - 1), 0)),
                     pl.BlockSpec((tm, d), lambda i: (jnp.maximum(i - n_ctx_tiles, 0), 0))]
        out_shape = [jax.ShapeDtypeStruct((rows.n_ctx, d), F32),
                     jax.ShapeDtypeStruct((n - rows.n_ctx, d), F32)]
    else:
        n_ctx_tiles = None
        out_specs = [row_tile, row_tile]
        out_shape = [jax.ShapeDtypeStruct((n, d), F32), jax.ShapeDtypeStruct((n, d), BF16)]
    return pl.pallas_call(
        functools.partial(_combine_kernel, n_ctx_tiles=n_ctx_tiles),
        grid=(n_tiles,),
        in_specs=[idx_spec(0), idx_spec(1),
                  row_tile,
                  pl.BlockSpec((tm, LANES), lambda i: (i, 0)),
                  _mod_spec(rows, layer, 5, tm),
                  gain_spec,
                  _mod_spec(rows, nxt, 0, tm),
                  _mod_spec(rows, nxt, 1, tm),
                  pl.BlockSpec(memory_space=pl.ANY)],
        out_specs=out_specs,
        out_shape=out_shape,
        scratch_shapes=[pltpu.VMEM((2, 2 * tm, d), F32), pltpu.SemaphoreType.DMA((2,))],
        compiler_params=_cp(1),
    )(dest3, dest3, x, wts, mod, gain, mod, mod, ys)


def kernel(x_prompt, x_sample, state_ret, state_hgrn, state_rglru, c, c_ctx, w_ada, b_ada, g_norm1, g_norm2, w_even_in, w_even_out, ret_decay, hg_lb_logits, g_ret_head, g_hg_head, w_odd_in, conv_w, conv_b, w_a, b_a, w_x, b_x, rg_lambda, w_odd_out, w_group, b_group, w_router, b_router, w1, w3, w2, g_final):
    bp, tp, d = x_prompt.shape
    bs, ts, _ = x_sample.shape
    depth = w_ada.shape[0]
    rows = _Rows(bp, tp, bs, ts)
    assert bs + 1 <= COND_ROWS

    lb_sm = jax.nn.softmax(hg_lb_logits.astype(F32), axis=0)
    lb_all = jnp.cumsum(lb_sm, axis=0) - lb_sm[0:1]
    log_gamma = -jnp.exp(ret_decay.astype(F32))
    softplus_neg_lam = jax.nn.softplus(-rg_lambda.astype(F32))
    t_idx = jnp.arange(ts)
    freqs = ROPE_BASE ** (-jnp.arange(HEAD_DIM // 4, dtype=F32) / (HEAD_DIM // 4))
    ang = jnp.concatenate([(t_idx // GRID_W).astype(F32)[:, None] * freqs,
                           (t_idx % GRID_W).astype(F32)[:, None] * freqs], axis=-1)
    rope_tabs = (jnp.concatenate([jnp.cos(ang), jnp.cos(ang)], axis=-1),
                 jnp.concatenate([-jnp.sin(ang), jnp.sin(ang)], axis=-1))
    w_route = jnp.concatenate(
        list(_split3(jnp.concatenate([w_group, w_router], axis=-1).astype(F32)))
        + [jnp.zeros((depth, d, LANES - 3 * ROUTE_COLS), BF16)], axis=-1)
    b_route = jnp.concatenate(
        [b_group, b_router, jnp.zeros((depth, LANES - ROUTE_COLS), F32)], axis=-1)
    bf = lambda w: w.astype(BF16)

    cond = jnp.zeros((COND_ROWS, d), F32).at[:bs].set(c).at[bs].set(c_ctx)
    mod = _adaln_all(cond, w_ada, b_ada)

    x = jnp.concatenate([x_prompt.reshape(bp * tp, d), x_sample.reshape(bs * ts, d)], axis=0)
    h = _norm0(x, g_norm1, mod, rows, 0)
    new_ret, new_hg, new_rg = [], [], []
    y_mix = jnp.zeros((rows.n, d), BF16)
    for l in range(depth):
        if l % 2 == 0:
            e = l // 2
            proj = _in_proj(h, w_even_in, e)
            y_mix, sr = _retention(proj, y_mix, log_gamma[e], g_ret_head[e], 0, bp, tp, None, None, True)
            y_mix, _ = _retention(proj, y_mix, log_gamma[e], g_ret_head[e], rows.n_ctx, bs, ts, rope_tabs,
                                  state_ret[:, e], False)
            y_mix, sh = _hgrn2(proj, y_mix, lb_all[e], g_hg_head[e], 0, bp, tp, None, True)
            y_mix, _ = _hgrn2(proj, y_mix, lb_all[e], g_hg_head[e], rows.n_ctx, bs, ts, state_hgrn[:, e], False)
            new_ret.append(sr)
            new_hg.append(sh)
            w_out = bf(w_even_out[e])
        else:
            o = l // 2
            proj = _in_proj(h, w_odd_in, o)
            gate_args = (conv_w[o], conv_b[o], bf(0.5 * w_a[o]), bf(0.5 * w_x[o]), 0.5 * b_a[o], 0.5 * b_x[o],
                         (-0.5 * RG_C) * softplus_neg_lam[o])
            y_mix, sg = _rglru(proj, y_mix, *gate_args, 0, bp, tp, None, True)
            y_mix, _ = _rglru(proj, y_mix, *gate_args, rows.n_ctx, bs, ts, state_rglru[:, o], False)
            new_rg.append(sg)
            w_out = bf(w_odd_out[o])
        x, h2, ids, wts = _outproj_route(y_mix, w_out, x, mod, g_norm2, w_route[l], b_route[l:l + 1], rows, l)
        tile_expert, n_used, src_rows, dest = _plan_routing(ids)
        ys = _moe_experts(h2, tile_expert, n_used, src_rows, w1, w3, w2, l)
        final = l == depth - 1
        x, h = _combine(ys, dest, wts, x, mod, g_final if final else g_norm1, rows, l, final)

    return (x.reshape(bp, tp, d), h.reshape(bs, ts, d), jnp.stack(new_ret, axis=1),
            jnp.stack(new_hg, axis=1), jnp.stack(new_rg, axis=1))
```

```python
import functools

import numpy as np
import jax
import jax.numpy as jnp
from jax import lax
from jax.experimental import pallas as pl
from jax.experimental.pallas import tpu as pltpu

F32 = jnp.float32
BF16 = jnp.bfloat16
HIGHEST = lax.Precision.HIGHEST

D_MODEL = 2048
GRID_W = 64
HEAD_DIM = 128
N_HEADS = 8
MIX_HALF = N_HEADS * HEAD_DIM
EVEN_IN = 9 * MIX_HALF
ROPE_BASE = 10000.0
RG_BLOCK = 128
RG_C = 8.0
N_GROUPS = 4
EXPERTS_PER_GROUP = 4
N_EXPERTS = N_GROUPS * EXPERTS_PER_GROUP
D_EXPERT = 512
EPS = 1e-6
F_MIN = 1e-20
COND_ROWS = 16
LANES = 128
SUBLANES = 8

RET_CHUNK = 128
RET_HEADS_PER_STEP = 2
HG_CHUNK = 128
HG_UNROLL = 2
HG_HEADS_PER_STEP = 2
HG_SPLIT = 2
ODD_CB = 1024
ROUTE_COLS = N_GROUPS + N_EXPERTS
MOE_TM = 256
VMEM_LIMIT = 52 * 1024 * 1024


def _cp(n_axes, vmem=VMEM_LIMIT):
    return pltpu.CompilerParams(dimension_semantics=("arbitrary",) * n_axes, vmem_limit_bytes=vmem)


def _silu(x):
    x_half = 0.5 * x
    return x_half + x_half * jnp.tanh(x_half)


def _dot(a, b, **kw):
    return jnp.dot(a, b, preferred_element_type=F32, **kw)


def _dot_nt(a, b):
    return lax.dot_general(a, b, (((1,), (1,)), ((), ())), preferred_element_type=F32)


def _dot_tn(a, b):
    return lax.dot_general(a, b, (((0,), (0,)), ((), ())), preferred_element_type=F32)


def _norm_mod(x, gain, scale, shift):
    ms = jnp.mean(x * x, axis=-1, keepdims=True)
    return x * lax.rsqrt(ms + EPS) * gain * (1.0 + scale) + shift


def _ada_kernel(cond_ref, w_ref, b_ref, o_ref):
    s3 = jnp.concatenate(_split3(_silu(cond_ref[...])), axis=0)
    w_hi, w_mid, _ = _split3(w_ref[...])
    r_hi, r_mid = _dot(s3, w_hi), _dot(s3, w_mid)
    n = COND_ROWS
    o_ref[...] = ((r_mid[0:n] + r_mid[n:2 * n] + r_hi[2 * n:3 * n])
                  + r_hi[n:2 * n] + r_hi[0:n] + b_ref[...])


def _adaln_all(cond, w_ada, b_ada):
    depth, d, n6 = w_ada.shape
    tn = 1024
    out = pl.pallas_call(
        _ada_kernel,
        grid=(depth, n6 // tn),
        in_specs=[pl.BlockSpec((COND_ROWS, d), lambda l, j: (0, 0)),
                  pl.BlockSpec((None, d, tn), lambda l, j: (l, 0, j)),
                  pl.BlockSpec((None, 1, tn), lambda l, j: (l, 0, j))],
        out_specs=pl.BlockSpec((None, COND_ROWS, tn), lambda l, j: (l, 0, j)),
        out_shape=jax.ShapeDtypeStruct((depth, COND_ROWS, n6), F32),
        compiler_params=_cp(2),
    )(cond, w_ada, b_ada.reshape(depth, 1, n6))
    return out.reshape(depth, COND_ROWS, 6, 1, d)


class _Rows:
    def __init__(self, bp, tp, bs, ts):
        self.bp, self.tp, self.bs, self.ts = bp, tp, bs, ts
        self.n_ctx = bp * tp
        self.n = self.n_ctx + bs * ts
        self.ctx_row = bs

    def cond_row(self, i, tm):
        n_ctx_tiles = self.n_ctx // tm
        per_seq = self.ts // tm
        return jnp.where(i < n_ctx_tiles, self.ctx_row, (i - n_ctx_tiles) // per_seq)

    def tile(self, cap):
        tm = min(cap, self.ts)
        assert self.ts % tm == 0 and self.n_ctx % tm == 0
        return tm


def _mod_spec(rows, layer, which, tm):
    d = D_MODEL
    return pl.BlockSpec((None, None, None, 1, d),
                        lambda i, *_: (layer, rows.cond_row(i, tm), which, 0, 0))


def _norm0_kernel(xc_ref, xl_ref, g_ref, sh_ref, sc_ref, x_ref, h_ref, *, n_ctx_tiles):
    def emit(x):
        x_ref[...] = x
        h_ref[...] = _norm_mod(x, g_ref[...], sc_ref[...], sh_ref[...]).astype(BF16)

    @pl.when(pl.program_id(0) < n_ctx_tiles)
    def _():
        emit(xc_ref[...])

    @pl.when(pl.program_id(0) >= n_ctx_tiles)
    def _():
        emit(xl_ref[...])


def _norm0(x_ctx, x_lat, g_norm, mod, rows, layer):
    d = x_ctx.shape[1]
    tm = rows.tile(512)
    n_ctx_tiles = rows.n_ctx // tm
    row_tile = pl.BlockSpec((tm, d), lambda i: (i, 0))
    return pl.pallas_call(
        functools.partial(_norm0_kernel, n_ctx_tiles=n_ctx_tiles),
        grid=(rows.n // tm,),
        in_specs=[pl.BlockSpec((tm, d), lambda i: (jnp.minimum(i, n_ctx_tiles - 1), 0)),
                  pl.BlockSpec((tm, d), lambda i: (jnp.maximum(i - n_ctx_tiles, 0), 0)),
                  pl.BlockSpec((None, 1, d), lambda i: (layer, 0, 0)),
                  _mod_spec(rows, layer, 0, tm),
                  _mod_spec(rows, layer, 1, tm)],
        out_specs=[row_tile, row_tile],
        out_shape=[jax.ShapeDtypeStruct((rows.n, d), F32), jax.ShapeDtypeStruct((rows.n, d), BF16)],
        compiler_params=_cp(1),
    )(x_ctx, x_lat, g_norm.reshape(-1, 1, d), mod, mod)


def _matmul_kernel(h_ref, w_ref, o_ref):
    o_ref[...] = _dot(h_ref[...], w_ref[...].astype(BF16))


def _in_proj(h, w, layer):
    n, k = h.shape
    _, _, nout = w.shape
    tm = 2048 if n % 2048 == 0 else 128
    tn = 512
    return pl.pallas_call(
        _matmul_kernel,
        grid=(n // tm, nout // tn),
        in_specs=[pl.BlockSpec((tm, k), lambda i, j: (i, 0)),
                  pl.BlockSpec((None, k, tn), lambda i, j: (layer, 0, j))],
        out_specs=pl.BlockSpec((tm, tn), lambda i, j: (i, j)),
        out_shape=jax.ShapeDtypeStruct((n, nout), F32),
        compiler_params=_cp(2),
    )(h, w)


def _ret_kernel(*refs, t, rope, has_state, emit_state):
    refs = list(refs)
    lg_ref = refs.pop(0)
    q_ref, k_ref, v_ref, g_ref, gain_ref = [refs.pop(0) for _ in range(5)]
    if rope:
        cos_ref, sin_ref = refs.pop(0), refs.pop(0)
    if has_state:
        s0_ref = refs.pop(0)
    refs.pop(0)
    o_ref = refs.pop(0)
    if emit_state:
        st_ref = refs.pop(0)
    sb_scr, qs, ks = refs

    c = min(RET_CHUNK, t)
    n_chunks = t // c
    hp = RET_HEADS_PER_STEP
    pos_c = lax.broadcasted_iota(jnp.int32, (c, 1), 0).astype(F32)
    pos_r = lax.broadcasted_iota(jnp.int32, (1, c), 1).astype(F32)
    rel = pos_c - pos_r
    full = jnp.full((1, HEAD_DIM), float(c), F32)

    def rows_of(i):
        return pl.ds(pl.multiple_of(i * c, c), c)

    for hh in range(hp):
        ln = pl.ds(hh * HEAD_DIM, HEAD_DIM)
        head = pl.program_id(1) * hp + hh
        lgf = lg_ref[0, head]
        lgb = lg_ref[1, head]

        q = q_ref[:, ln]
        k = k_ref[:, ln]
        if rope:
            cs, sn = cos_ref[...], sin_ref[...]
            q = q * cs + pltpu.roll(q, HEAD_DIM // 2, 1) * sn
            k = k * cs + pltpu.roll(k, HEAD_DIM // 2, 1) * sn
        qs[:, ln] = q
        ks[:, ln] = k * (HEAD_DIM ** -0.5)

        decay = (jnp.where(rel >= 0, jnp.exp(lgf * jnp.maximum(rel, 0.0)), 0.0)
                 + jnp.where(rel <= 0, jnp.exp(lgb * jnp.maximum(-rel, 0.0)), 0.0))
        qdf = jnp.exp(lgf * (pos_c + 1.0))
        kdf = jnp.exp(lgf * (c - 1.0 - pos_c))
        qdb = jnp.exp(lgb * (c - pos_c))
        kdb = jnp.exp(lgb * pos_c)
        cdf = jnp.exp(lgf * full)
        cdb = jnp.exp(lgb * full)

        def bwd_body(ii, s, ln=ln, hh=hh, kdb=kdb, cdb=cdb):
            i = n_chunks - 1 - ii
            sb_scr[hh * n_chunks + i] = s
            sl = rows_of(i)
            kv = _dot_tn((ks[sl, ln] * kdb).astype(BF16), v_ref[sl, ln].astype(BF16))
            return s * cdb + kv

        s0_b = s0_ref[1, hh] if has_state else jnp.zeros((HEAD_DIM, HEAD_DIM), F32)
        s_b = lax.fori_loop(0, n_chunks, bwd_body, s0_b, unroll=True)

        gain = gain_ref[:, ln]

        def fwd_body(i, s, ln=ln, hh=hh, decay=decay, qdf=qdf, qdb=qdb, kdf=kdf, cdf=cdf, gain=gain):
            sl = rows_of(i)
            qc, kc = qs[sl, ln], ks[sl, ln]
            vb = v_ref[sl, ln].astype(BF16)
            scores = _dot_nt(qc.astype(BF16), kc.astype(BF16)) * decay
            o = (_dot(scores.astype(BF16), vb)
                 + _dot((qc * qdf).astype(BF16), s.astype(BF16))
                 + _dot((qc * qdb).astype(BF16), sb_scr[hh * n_chunks + i].astype(BF16)))
            ms = jnp.mean(o * o, axis=-1, keepdims=True)
            o = o * lax.rsqrt(ms + EPS) * gain
            o_ref[sl, ln] = (o * _silu(g_ref[sl, ln])).astype(BF16)
            return s * cdf + _dot_tn((kc * kdf).astype(BF16), vb)

        s0_f = s0_ref[0, hh] if has_state else jnp.zeros((HEAD_DIM, HEAD_DIM), F32)
        s_f = lax.fori_loop(0, n_chunks, fwd_body, s0_f, unroll=True)
        if emit_state:
            st_ref[0, hh] = s_f
            st_ref[1, hh] = s_b


def _retention(proj, y_mix, log_gamma, g_head, row0, b, t, rope_tabs, s0, emit_state):
    rb0 = row0 // t
    hp = RET_HEADS_PER_STEP
    lanes = hp * HEAD_DIM
    hblocks = N_HEADS // hp
    blk = lambda col0: pl.BlockSpec((t, lanes), lambda bi, h: (rb0 + bi, col0 + h))
    state_spec = pl.BlockSpec((None, 2, hp, HEAD_DIM, HEAD_DIM), lambda bi, h: (bi, 0, h, 0, 0))
    in_specs = [pl.BlockSpec(memory_space=pltpu.SMEM),
                blk(0), blk(hblocks), blk(2 * hblocks), blk(3 * hblocks),
                pl.BlockSpec((None, 1, lanes), lambda bi, h: (h, 0, 0))]
    args = [log_gamma, proj, proj, proj, proj, g_head.reshape(hblocks, 1, lanes)]
    if rope_tabs is not None:
        in_specs += [pl.BlockSpec((t, HEAD_DIM), lambda bi, h: (0, 0))] * 2
        args += list(rope_tabs)
    if s0 is not None:
        in_specs.append(state_spec)
        args.append(s0)
    in_specs.append(pl.BlockSpec(memory_space=pl.ANY))
    args.append(y_mix)
    alias_idx = len(args) - 1
    out_specs = [pl.BlockSpec((t, lanes), lambda bi, h: (rb0 + bi, h))]
    out_shape = [jax.ShapeDtypeStruct(y_mix.shape, y_mix.dtype)]
    if emit_state:
        out_specs.append(state_spec)
        out_shape.append(jax.ShapeDtypeStruct((b, 2, N_HEADS, HEAD_DIM, HEAD_DIM), F32))
    n_chunks = t // min(RET_CHUNK, t)
    outs = pl.pallas_call(
        functools.partial(_ret_kernel, t=t, rope=rope_tabs is not None, has_state=s0 is not None,
                          emit_state=emit_state),
        grid=(b, hblocks),
        in_specs=in_specs,
        out_specs=out_specs,
        out_shape=out_shape,
        scratch_shapes=[pltpu.VMEM((hp * n_chunks, HEAD_DIM, HEAD_DIM), F32),
                        pltpu.VMEM((t, lanes), F32),
                        pltpu.VMEM((t, lanes), F32)],
        input_output_aliases={alias_idx: 0},
        compiler_params=_cp(2),
    )(*args)
    return outs[0], (outs[1] if emit_state else None)


def _hg_plan(c):
    t = np.arange(c)
    lower = (t[:, None] >= t[None, :]).astype(np.float32)
    stacks, masks = [], []
    for fwd in (True, False):
        tri = lower if fwd else lower.T
        end = c - 1 if fwd else 0
        blocks = [tri, tri[end:end + 1, :] - tri]
        level_masks = []
        w = 1
        while w < c:
            blk = t // (2 * w)
            late = (t % (2 * w)) >= w
            query = late if fwd else ~late
            ref = blk * 2 * w + (w - 1 if fwd else w)
            blocks.append(np.where(query, 1.0, -1.0)[:, None] * (tri - tri[ref, :]))
            level_masks.append(((blk[:, None] == blk[None, :]) & query[:, None] & ~query[None, :])
                               .astype(np.float32))
            w *= 2
        m = np.concatenate(blocks, axis=0)
        stacks.append(np.concatenate([m] * HG_SPLIT, axis=1))
        masks.append(np.stack(level_masks))
    return np.stack(stacks), np.stack(masks)


def _split3(x):
    hi = x.astype(BF16)
    r1 = x - hi.astype(F32)
    mid = r1.astype(BF16)
    lo = (r1 - mid.astype(F32)).astype(BF16)
    return hi, mid, lo


def _hg_kernel(*refs, t, has_state, emit_state):
    refs = list(refs)
    (q_ref, ff_ref, fb_ref, v_ref, gate_ref, lb_ref, gain_ref, m_ref, mask_ref) = [refs.pop(0) for _ in range(9)]
    if has_state:
        s0_ref = refs.pop(0)
    refs.pop(0)
    o_ref = refs.pop(0)
    if emit_state:
        st_ref = refs.pop(0)
    qa, k_fw, l_fw, k_bw, l_bw, o_fw, o_bw, st = refs

    c = min(HG_CHUNK, t)
    n_chunks = t // c
    n_levels = mask_ref.shape[1]
    hp = HG_HEADS_PER_STEP
    heads = [slice(hh * HEAD_DIM, (hh + 1) * HEAD_DIM) for hh in range(hp)]
    lbh = lb_ref[...]
    qa[...] = _silu(q_ref[...])

    def forget(z_ref, k_out, l_out):
        sg = jax.nn.sigmoid(z_ref[...])
        l_out[...] = jnp.log(jnp.maximum(lbh + (1.0 - lbh) * sg, F_MIN))
        k_out[...] = (1.0 - lbh) * (1.0 - sg)

    forget(ff_ref, k_fw, l_fw)
    forget(fb_ref, k_bw, l_bw)

    for d in range(2):
        for hh in range(hp):
            st[d, hh] = s0_ref[d, hh].T if has_state else jnp.zeros((HEAD_DIM, HEAD_DIM), F32)

    row = lax.broadcasted_iota(jnp.int32, (c, c), 0)
    col = lax.broadcasted_iota(jnp.int32, (c, c), 1)

    def rows_of(i):
        return pl.ds(pl.multiple_of(i * c, c), c)

    def direction(d, sl, own):
        q, k, vb = qa[sl, :], (k_fw, k_bw)[d][sl, :], v_ref[sl, :].astype(BF16)
        pieces = _split3((l_fw, l_bw)[d][sl, :])[:HG_SPLIT]
        e = jnp.exp(_dot(m_ref[d], jnp.concatenate(pieces, axis=0)))
        end = c - 1 if d == 0 else 0
        outs = []
        for hh, ln in enumerate(heads):
            qh, kh, vh = q[:, ln], k[:, ln], vb[:, ln]
            p = own[hh] if own is not None else jnp.zeros((c, c), F32)
            for j in range(n_levels):
                ej = e[(2 + j) * c:(3 + j) * c, ln]
                p = p + _dot_nt((qh * ej).astype(BF16), (kh * ej).astype(BF16)) * mask_ref[d, j]
            s = st[d, hh]
            outs.append(_dot(p.astype(BF16), vh) + _dot_nt((qh * e[0:c, ln]).astype(BF16), s.astype(BF16)))
            st[d, hh] = s * e[end:end + 1, ln] + _dot_tn(vh, (kh * e[c:2 * c, ln]).astype(BF16))
        return jnp.concatenate(outs, axis=1)

    def body(i, carry):
        sl = rows_of(i)
        k_both = (k_fw[sl, :] + k_bw[sl, :]).astype(BF16)
        q = qa[sl, :].astype(BF16)
        own = [jnp.where(row == col, _dot_nt(q[:, ln], k_both[:, ln]), 0.0) for ln in heads]
        o_fw[sl, :] = direction(0, sl, own)
        sl = rows_of(n_chunks - 1 - i)
        o_bw[sl, :] = direction(1, sl, None)
        return carry

    lax.fori_loop(0, n_chunks, body, 0, unroll=min(HG_UNROLL, n_chunks))
    gate = _silu(gate_ref[...])
    gain = gain_ref[...]
    for ln in heads:
        o = o_fw[:, ln] + o_bw[:, ln]
        ms = jnp.mean(o * o, axis=-1, keepdims=True)
        o_ref[:, ln] = (o * lax.rsqrt(ms + EPS) * gain[:, ln] * gate[:, ln]).astype(BF16)
    if emit_state:
        for d in range(2):
            for hh in range(hp):
                st_ref[d, hh] = st[d, hh].T


def _hgrn2(proj, y_mix, lb, g_head, row0, b, t, s0, emit_state):
    rb0 = row0 // t
    c = min(HG_CHUNK, t)
    hp = HG_HEADS_PER_STEP
    lanes = hp * HEAD_DIM
    m_np, mask_np = _hg_plan(c)
    m_stack = jnp.asarray(m_np, BF16)
    lvl_mask = jnp.asarray(mask_np, F32)
    hblocks = N_HEADS // hp
    blk = lambda col0: pl.BlockSpec((t, lanes), lambda bi, h: (rb0 + bi, col0 + h))
    per_head = pl.BlockSpec((None, 1, lanes), lambda bi, h: (h, 0, 0))
    state_spec = pl.BlockSpec((None, 2, hp, HEAD_DIM, HEAD_DIM), lambda bi, h: (bi, 0, h, 0, 0))
    in_specs = [blk(4 * hblocks), blk(5 * hblocks), blk(6 * hblocks), blk(7 * hblocks), blk(8 * hblocks),
                per_head, per_head,
                pl.BlockSpec(m_stack.shape, lambda bi, h: (0, 0, 0)),
                pl.BlockSpec(lvl_mask.shape, lambda bi, h: (0, 0, 0, 0))]
    args = [proj] * 5 + [lb.reshape(hblocks, 1, lanes), g_head.reshape(hblocks, 1, lanes),
                         m_stack, lvl_mask]
    if s0 is not None:
        in_specs.append(state_spec)
        args.append(s0)
    in_specs.append(pl.BlockSpec(memory_space=pl.ANY))
    args.append(y_mix)
    alias_idx = len(args) - 1
    out_specs = [pl.BlockSpec((t, lanes), lambda bi, h: (rb0 + bi, hblocks + h))]
    out_shape = [jax.ShapeDtypeStruct(y_mix.shape, y_mix.dtype)]
    if emit_state:
        out_specs.append(state_spec)
        out_shape.append(jax.ShapeDtypeStruct((b, 2, N_HEADS, HEAD_DIM, HEAD_DIM), F32))
    seq = pltpu.VMEM((t, lanes), F32)
    outs = pl.pallas_call(
        functools.partial(_hg_kernel, t=t, has_state=s0 is not None, emit_state=emit_state),
        grid=(b, hblocks),
        in_specs=in_specs,
        out_specs=out_specs,
        out_shape=out_shape,
        scratch_shapes=[seq] * 7 + [pltpu.VMEM((2, hp, HEAD_DIM, HEAD_DIM), F32)],
        input_output_aliases={alias_idx: 0},
        compiler_params=_cp(2),
    )(*args)
    return outs[0], (outs[1] if emit_state else None)


def _block_scan(a, b, reverse):
    row = lax.broadcasted_iota(jnp.int32, a.shape, 0)
    k = 1
    while k < SUBLANES:
        shift = (SUBLANES - k) if reverse else k
        valid = (row < SUBLANES - k) if reverse else (row >= k)
        a_prev = pltpu.roll(a, shift, 0)
        b_prev = pltpu.roll(b, shift, 0)
        b = jnp.where(valid, a * b_prev + b, b)
        a = jnp.where(valid, a * a_prev, a)
        k *= 2
    return a, b


def _odd_kernel(*refs, t, has_state, emit_state):
    refs = list(refs)
    (gi_ref, xb_ref, cw_ref, cb_ref, wa_ref, wx_ref, ba_ref, bx_ref, sp_ref) = [refs.pop(0) for _ in range(9)]
    if has_state:
        s0_ref = refs.pop(0)
    refs.pop(0)
    o_ref = refs.pop(0)
    if emit_state:
        st_ref = refs.pop(0)
    xpad, xc, a_f, b_f, a_b, b_b = refs

    cb = xb_ref.shape[1]
    pad = SUBLANES
    zeros = jnp.zeros((pad, cb), F32)
    xpad[pl.ds(0, pad), :] = zeros
    xpad[pl.ds(pad + t, pad), :] = zeros
    xpad[pl.ds(pad, t), :] = xb_ref[...]
    for n in range(cb // RG_BLOCK):
        cols = pl.ds(n * RG_BLOCK, RG_BLOCK)
        xc[:, cols] = (cb_ref[:, cols]
                       + xpad[pl.ds(pad - 1, t), cols] * cw_ref[0:1, cols]
                       + xpad[pl.ds(pad, t), cols] * cw_ref[1:2, cols]
                       + xpad[pl.ds(pad + 1, t), cols] * cw_ref[2:3, cols]
                       + xpad[pl.ds(pad + 2, t), cols] * cw_ref[3:4, cols])

    rc = min(256, t)

    def gate_body(i, carry):
        sl = pl.ds(pl.multiple_of(i * rc, rc), rc)
        for n in range(cb // RG_BLOCK):
            cols = pl.ds(n * RG_BLOCK, RG_BLOCK)
            x_blk = xc[sl, cols]
            x_bf = x_blk.astype(BF16)
            x_half = 0.5 * x_blk
            for d, (a_out, b_out) in enumerate(((a_f, b_f), (a_b, b_b))):
                t_r = jnp.tanh(_dot(x_bf, wa_ref[d, n]) + ba_ref[d, :, cols])
                t_i = jnp.tanh(_dot(x_bf, wx_ref[d, n]) + bx_ref[d, :, cols])
                log_a = sp_ref[d, :, cols] * (1.0 + t_r)
                a = jnp.exp(log_a)
                a_out[sl, cols] = a
                z = jnp.maximum(-jnp.tanh(log_a) * (1.0 + a * a), F_MIN)
                b_out[sl, cols] = (z * lax.rsqrt(z)) * (x_half + x_half * t_i)
        return carry

    lax.fori_loop(0, t // rc, gate_body, 0)

    n_blocks = t // SUBLANES

    def rows_of(j):
        return pl.ds(pl.multiple_of(j * SUBLANES, SUBLANES), SUBLANES)

    def scan_body(j, carry):
        h_prev, h_next = carry
        sl = rows_of(j)
        a, b = _block_scan(a_f[sl, :], b_f[sl, :], False)
        h = a * h_prev + b
        b_f[sl, :] = h
        h_prev = jnp.broadcast_to(h[SUBLANES - 1:SUBLANES, :], h.shape)
        sl = rows_of(n_blocks - 1 - j)
        a, b = _block_scan(a_b[sl, :], b_b[sl, :], True)
        h = a * h_next + b
        b_b[sl, :] = h
        return h_prev, jnp.broadcast_to(h[0:1, :], h.shape)

    h0_f = s0_ref[0] if has_state else jnp.zeros((1, cb), F32)
    h0_b = s0_ref[1] if has_state else jnp.zeros((1, cb), F32)
    last_f, first_b = lax.fori_loop(
        0, n_blocks, scan_body,
        (jnp.broadcast_to(h0_f, (SUBLANES, cb)), jnp.broadcast_to(h0_b, (SUBLANES, cb))), unroll=2)

    def out_body(i, carry):
        sl = pl.ds(pl.multiple_of(i * rc, rc), rc)
        gate = jax.nn.gelu(gi_ref[sl, :], approximate=True)
        o_ref[sl, :] = ((b_f[sl, :] + b_b[sl, :]) * gate).astype(BF16)
        return carry

    lax.fori_loop(0, t // rc, out_body, 0)
    if emit_state:
        st_ref[0] = last_f[0:1, :]
        st_ref[1] = first_b[0:1, :]


def _rglru(proj, y_mix, conv_w, conv_b, w_a, w_x, b_a, b_x, softplus_neg_lam, row0, b, t, s0, emit_state):
    d = D_MODEL
    cb = ODD_CB
    ncb = d // cb
    nrb = cb // RG_BLOCK
    rb0 = row0 // t
    vec = lambda rows: pl.BlockSpec((rows, cb), lambda bi, j: (0, j))
    vec2 = pl.BlockSpec((2, 1, cb), lambda bi, j: (0, 0, j))
    wspec = pl.BlockSpec((2, nrb, RG_BLOCK, RG_BLOCK), lambda bi, j: (0, j, 0, 0))
    in_specs = [pl.BlockSpec((t, cb), lambda bi, j: (rb0 + bi, j)),
                pl.BlockSpec((t, cb), lambda bi, j: (rb0 + bi, ncb + j)),
                vec(4), vec(1), wspec, wspec, vec2, vec2, vec2]
    args = [proj, proj, conv_w, conv_b.reshape(1, d), w_a, w_x,
            b_a.reshape(2, 1, d), b_x.reshape(2, 1, d), softplus_neg_lam.reshape(2, 1, d)]
    if s0 is not None:
        in_specs.append(pl.BlockSpec((None, 2, 1, cb), lambda bi, j: (bi, 0, 0, j)))
        args.append(s0.reshape(b, 2, 1, d))
    in_specs.append(pl.BlockSpec(memory_space=pl.ANY))
    args.append(y_mix)
    alias_idx = len(args) - 1
    out_specs = [pl.BlockSpec((t, cb), lambda bi, j: (rb0 + bi, j))]
    out_shape = [jax.ShapeDtypeStruct(y_mix.shape, y_mix.dtype)]
    if emit_state:
        out_specs.append(pl.BlockSpec((None, 2, 1, cb), lambda bi, j: (bi, 0, 0, j)))
        out_shape.append(jax.ShapeDtypeStruct((b, 2, 1, d), F32))
    seq = pltpu.VMEM((t, cb), F32)
    outs = pl.pallas_call(
        functools.partial(_odd_kernel, t=t, has_state=s0 is not None, emit_state=emit_state),
        grid=(b, ncb),
        in_specs=in_specs,
        out_specs=out_specs,
        out_shape=out_shape,
        scratch_shapes=[pltpu.VMEM((t + 2 * SUBLANES, cb), F32), seq, seq, seq, seq, seq],
        input_output_aliases={alias_idx: 0},
        compiler_params=_cp(2),
    )(*args)
    return outs[0], (outs[1].reshape(b, 2, d) if emit_state else None)


def _route(logits):
    lane = lax.broadcasted_iota(jnp.int32, logits.shape, 1)
    neg = jnp.float32(-jnp.inf)
    big = jnp.int32(LANES)

    def arg_max(vals):
        m = jnp.max(vals, axis=-1, keepdims=True)
        return m, jnp.min(jnp.where(vals == m, lane, big), axis=-1, keepdims=True)

    g_logits = jnp.where(lane < N_GROUPS, logits, neg)
    g_max, g_sel = arg_max(g_logits)
    p_grp = 1.0 / jnp.sum(jnp.exp(g_logits - g_max), axis=-1, keepdims=True)
    lo = N_GROUPS + EXPERTS_PER_GROUP * g_sel
    e_logits = jnp.where((lane >= lo) & (lane < lo + EXPERTS_PER_GROUP), logits, neg)
    v1, i1 = arg_max(e_logits)
    v2, i2 = arg_max(jnp.where(lane == i1, neg, e_logits))
    e2 = jnp.exp(v2 - v1)
    w1 = p_grp / (1.0 + e2)
    w2 = p_grp * e2 / (1.0 + e2)
    ids = jnp.where(lane == 0, i1 - N_GROUPS, jnp.where(lane == 1, i2 - N_GROUPS, 0))
    wts = jnp.where(lane == 0, w1, jnp.where(lane == 1, w2, 0.0))
    return ids, wts


def _outproj_kernel(y_ref, w_ref, x_ref, g1_ref, gn_ref, sh_ref, sc_ref, wr_ref, br_ref,
                    xo_ref, h_ref, ids_ref, wts_ref):
    x = x_ref[...] + g1_ref[...] * _dot(y_ref[...], w_ref[...])
    xo_ref[...] = x
    h = _norm_mod(x, gn_ref[...], sc_ref[...], sh_ref[...])
    h_ref[...] = h
    wr = wr_ref[...]
    r_hi, r_mid, r_lo = [_dot(piece, wr) for piece in _split3(h)]
    to_mid, to_lo = LANES - ROUTE_COLS, LANES - 2 * ROUTE_COLS
    small = r_lo + pltpu.roll(r_mid, to_mid, 1) + pltpu.roll(r_hi, to_lo, 1)
    logits = ((small + r_mid) + pltpu.roll(r_hi, to_mid, 1)) + r_hi + br_ref[...]
    ids, wts = _route(logits)
    ids_ref[...] = ids
    wts_ref[...] = wts


def _outproj_route(y_mix, w_out, x, mod, g_norm2, w_route, b_route, rows, layer):
    n, d = x.shape
    tm = rows.tile(512)
    row_tile = pl.BlockSpec((tm, d), lambda i: (i, 0))
    slab = pl.BlockSpec((tm, LANES), lambda i: (i, 0))
    return pl.pallas_call(
        _outproj_kernel,
        grid=(n // tm,),
        in_specs=[row_tile,
                  pl.BlockSpec((d, d), lambda i: (0, 0), pipeline_mode=pl.Buffered(1)),
                  row_tile,
                  _mod_spec(rows, layer, 2, tm),
                  pl.BlockSpec((None, 1, d), lambda i: (layer, 0, 0)),
                  _mod_spec(rows, layer, 3, tm),
                  _mod_spec(rows, layer, 4, tm),
                  pl.BlockSpec((d, LANES), lambda i: (0, 0)),
                  pl.BlockSpec((1, LANES), lambda i: (0, 0))],
        out_specs=[row_tile, row_tile, slab, slab],
        out_shape=[jax.ShapeDtypeStruct((n, d), F32), jax.ShapeDtypeStruct((n, d), F32),
                   jax.ShapeDtypeStruct((n, LANES), jnp.int32), jax.ShapeDtypeStruct((n, LANES), F32)],
        compiler_params=_cp(1),
    )(y_mix, w_out, x, mod, g_norm2.reshape(-1, 1, d), mod, mod, w_route, b_route)


def _row_copy(src_hbm, dst, sem, r, src_row):
    return pltpu.make_async_copy(src_hbm.at[pl.ds(src_row, 1)], dst.at[pl.ds(r, 1)], sem)


def _gather_start(idx_ref, n_rows, src_hbm, dst, sem):
    for r in range(n_rows):
        _row_copy(src_hbm, dst, sem, r, idx_ref[0, 0, r]).start(priority=r % 2)


def _gather_wait(n_rows, src_hbm, dst, sem):
    def body(r, carry):
        _row_copy(src_hbm, dst, sem, r, 0).wait()
        return carry

    lax.fori_loop(0, n_rows, body, 0, unroll=True)


def _moe_kernel(tile_e_ref, n_used_ref, cur_ref, nxt_ref, h_hbm, w1_ref, w3_ref, w2_ref, o_ref,
                xbuf, sem, w1_bf, w3_bf, w2_bf):
    i = pl.program_id(0)
    n_used = n_used_ref[0]
    slot = lax.rem(i, 2)
    tm = xbuf.shape[1]

    new_expert = (i == 0) | (tile_e_ref[i] != tile_e_ref[jnp.maximum(i - 1, 0)])

    @pl.when((i < n_used) & new_expert)
    def _():
        w1_bf[...] = w1_ref[...].astype(BF16)
        w3_bf[...] = w3_ref[...].astype(BF16)
        w2_bf[...] = w2_ref[...].astype(BF16)

    @pl.when(i == 0)
    def _():
        _gather_start(cur_ref, tm, h_hbm, xbuf.at[0], sem.at[0])

    @pl.when(i + 1 < n_used)
    def _():
        _gather_start(nxt_ref, tm, h_hbm, xbuf.at[1 - slot], sem.at[1 - slot])

    @pl.when(i < n_used)
    def _():
        _gather_wait(tm, h_hbm, xbuf.at[slot], sem.at[slot])
        x = xbuf[slot].astype(BF16)
        hid = _silu(_dot(x, w1_bf[...])) * _dot(x, w3_bf[...])
        o_ref[...] = _dot(hid.astype(BF16), w2_bf[...])

    @pl.when(i >= n_used)
    def _():
        o_ref[...] = jnp.zeros(o_ref.shape, F32)


def _moe_experts(h2, tile_expert, n_used, src_rows, w1, w3, w2, layer):
    n, d = h2.shape
    n_tiles = tile_expert.shape[0]
    tm = MOE_TM
    r = n_tiles * tm
    idx_spec = lambda step: pl.BlockSpec(
        (1, 1, tm), lambda i, te, nu: (jnp.minimum(i + step, n_tiles - 1), 0, 0), memory_space=pltpu.SMEM)
    w_spec = lambda rows_, cols_: pl.BlockSpec((None, None, rows_, cols_),
                                               lambda i, te, nu: (layer, te[i], 0, 0))
    grid_spec = pltpu.PrefetchScalarGridSpec(
        num_scalar_prefetch=2,
        grid=(n_tiles,),
        in_specs=[idx_spec(0), idx_spec(1),
                  pl.BlockSpec(memory_space=pl.ANY),
                  w_spec(d, D_EXPERT), w_spec(d, D_EXPERT), w_spec(D_EXPERT, d)],
        out_specs=pl.BlockSpec((tm, d), lambda i, te, nu: (i, 0)),
        scratch_shapes=[pltpu.VMEM((2, tm, d), F32), pltpu.SemaphoreType.DMA((2,)),
                        pltpu.VMEM((d, D_EXPERT), BF16), pltpu.VMEM((d, D_EXPERT), BF16),
                        pltpu.VMEM((D_EXPERT, d), BF16)],
    )
    src3 = src_rows.reshape(n_tiles, 1, tm)
    return pl.pallas_call(
        _moe_kernel,
        grid_spec=grid_spec,
        out_shape=jax.ShapeDtypeStruct((r, d), F32),
        compiler_params=_cp(1),
    )(tile_expert, n_used, src3, src3, h2, w1, w3, w2)


def _invert_kernel(dest_ref, zeros_hbm, src_ref):
    step = pl.program_id(0)
    chunk = dest_ref.shape[2]

    @pl.when(step == 0)
    def _():
        pltpu.sync_copy(zeros_hbm, src_ref)

    base = step * chunk

    def body(j, carry):
        tok = lax.shift_right_logical(base, 1) + j
        src_ref[dest_ref[0, 0, 2 * j]] = tok
        src_ref[dest_ref[0, 0, 2 * j + 1]] = tok
        return carry

    lax.fori_loop(0, chunk // 2, body, 0, unroll=8)


def _invert_plan(dest_flat, n_slots):
    n_assign = dest_flat.shape[0]
    chunk = 2048 if n_assign % 2048 == 0 else n_assign
    assert chunk % 2 == 0
    return pl.pallas_call(
        _invert_kernel,
        grid=(n_assign // chunk,),
        in_specs=[pl.BlockSpec((1, 1, chunk), lambda s: (s, 0, 0), memory_space=pltpu.SMEM),
                  pl.BlockSpec(memory_space=pl.ANY)],
        out_specs=pl.BlockSpec(memory_space=pltpu.SMEM),
        out_shape=jax.ShapeDtypeStruct((n_slots,), jnp.int32),
        compiler_params=_cp(1),
    )(dest_flat.reshape(-1, 1, chunk), jnp.zeros((n_slots,), jnp.int32))


def _plan_routing(ids):
    n = ids.shape[0]
    tm = MOE_TM
    n_tiles = (2 * n) // tm + N_EXPERTS
    r = n_tiles * tm
    e_flat = ids[:, :2].reshape(-1)
    onehot = (e_flat[:, None] == jnp.arange(N_EXPERTS, dtype=jnp.int32)[None, :]).astype(jnp.int32)
    csum = jnp.cumsum(onehot, axis=0)
    rank = jnp.sum(onehot * csum, axis=1) - 1
    counts = csum[-1]
    padded = ((counts + tm - 1) // tm) * tm
    ends = jnp.cumsum(padded)
    offs = ends - padded
    dest = jnp.sum(onehot * offs[None, :], axis=1) + rank
    tile_start = jnp.arange(n_tiles, dtype=jnp.int32) * tm
    tile_expert = jnp.minimum(jnp.sum((tile_start[:, None] >= ends[None, :]).astype(jnp.int32), axis=1),
                              N_EXPERTS - 1).astype(jnp.int32)
    src_rows = _invert_plan(dest.astype(jnp.int32), r)
    n_used = (ends[-1:] // tm).astype(jnp.int32)
    return tile_expert, n_used, src_rows, dest.reshape(n, 2)


def _combine_kernel(cur_ref, nxt_ref, x_ref, wts_ref, g2_ref, gn_ref, sh_ref, sc_ref, ys_hbm,
                    out_a, out_b, buf, sem, *, n_ctx_tiles):
    i = pl.program_id(0)
    slot = lax.rem(i, 2)
    tm = x_ref.shape[0]

    @pl.when(i == 0)
    def _():
        _gather_start(cur_ref, 2 * tm, ys_hbm, buf.at[0], sem.at[0])

    @pl.when(i + 1 < pl.num_programs(0))
    def _():
        _gather_start(nxt_ref, 2 * tm, ys_hbm, buf.at[1 - slot], sem.at[1 - slot])

    _gather_wait(2 * tm, ys_hbm, buf.at[slot], sem.at[slot])
    wts = wts_ref[...]
    moe = buf[slot, pl.ds(0, tm), :] * wts[:, 0:1] + buf[slot, pl.ds(tm, tm), :] * wts[:, 1:2]
    x = x_ref[...] + g2_ref[...] * moe
    if n_ctx_tiles is None:
        out_a[...] = x
        out_b[...] = _norm_mod(x, gn_ref[...], sc_ref[...], sh_ref[...]).astype(BF16)
    else:
        ms = jnp.mean(x * x, axis=-1, keepdims=True)
        y = x * lax.rsqrt(ms + EPS) * gn_ref[...]

        @pl.when(i < n_ctx_tiles)
        def _():
            out_a[...] = y

        @pl.when(i >= n_ctx_tiles)
        def _():
            out_b[...] = y


def _combine(ys, dest, wts, x, mod, g_next, rows, layer, final):
    n, d = x.shape
    tm = rows.tile(256)
    n_tiles = n // tm
    dest3 = dest.reshape(n_tiles, tm, 2).transpose(0, 2, 1).reshape(n_tiles, 1, 2 * tm)
    idx_spec = lambda step: pl.BlockSpec(
        (1, 1, 2 * tm), lambda i: (jnp.minimum(i + step, n_tiles - 1), 0, 0), memory_space=pltpu.SMEM)
    row_tile = pl.BlockSpec((tm, d), lambda i: (i, 0))
    nxt = layer if final else layer + 1
    gain_spec = (pl.BlockSpec((1, d), lambda i: (0, 0)) if final
                 else pl.BlockSpec((None, 1, d), lambda i: (nxt, 0, 0)))
    gain = g_next.reshape(1, d) if final else g_next.reshape(-1, 1, d)
    if final:
        n_ctx_tiles = rows.n_ctx // tm
        out_specs = [pl.BlockSpec((tm, d), lambda i: (jnp.minimum(i, n_ctx_tiles - 1), 0)),
                     pl.BlockSpec((tm, d), lambda i: (jnp.maximum(i - n_ctx_tiles, 0), 0))]
        out_shape = [jax.ShapeDtypeStruct((rows.n_ctx, d), F32),
                     jax.ShapeDtypeStruct((n - rows.n_ctx, d), F32)]
    else:
        n_ctx_tiles = None
        out_specs = [row_tile, row_tile]
        out_shape = [jax.ShapeDtypeStruct((n, d), F32), jax.ShapeDtypeStruct((n, d), BF16)]
    return pl.pallas_call(
        functools.partial(_combine_kernel, n_ctx_tiles=n_ctx_tiles),
        grid=(n_tiles,),
        in_specs=[idx_spec(0), idx_spec(1),
                  row_tile,
                  pl.BlockSpec((tm, LANES), lambda i: (i, 0)),
                  _mod_spec(rows, layer, 5, tm),
                  gain_spec,
                  _mod_spec(rows, nxt, 0, tm),
                  _mod_spec(rows, nxt, 1, tm),
                  pl.BlockSpec(memory_space=pl.ANY)],
        out_specs=out_specs,
        out_shape=out_shape,
        scratch_shapes=[pltpu.VMEM((2, 2 * tm, d), F32), pltpu.SemaphoreType.DMA((2,))],
        compiler_params=_cp(1),
    )(dest3, dest3, x, wts, mod, gain, mod, mod, ys)


def kernel(x_prompt, x_sample, state_ret, state_hgrn, state_rglru, c, c_ctx, w_ada, b_ada, g_norm1, g_norm2, w_even_in, w_even_out, ret_decay, hg_lb_logits, g_ret_head, g_hg_head, w_odd_in, conv_w, conv_b, w_a, b_a, w_x, b_x, rg_lambda, w_odd_out, w_group, b_group, w_router, b_router, w1, w3, w2, g_final):
    bp, tp, d = x_prompt.shape
    bs, ts, _ = x_sample.shape
    depth = w_ada.shape[0]
    rows = _Rows(bp, tp, bs, ts)
    assert bs + 1 <= COND_ROWS

    lb_sm = jax.nn.softmax(hg_lb_logits.astype(F32), axis=0)
    lb_all = jnp.cumsum(lb_sm, axis=0) - lb_sm[0:1]
    log_gamma = -jnp.exp(ret_decay.astype(F32))
    softplus_neg_lam = jax.nn.softplus(-rg_lambda.astype(F32))
    t_idx = jnp.arange(ts)
    freqs = ROPE_BASE ** (-jnp.arange(HEAD_DIM // 4, dtype=F32) / (HEAD_DIM // 4))
    ang = jnp.concatenate([(t_idx // GRID_W).astype(F32)[:, None] * freqs,
                           (t_idx % GRID_W).astype(F32)[:, None] * freqs], axis=-1)
    rope_tabs = (jnp.concatenate([jnp.cos(ang), jnp.cos(ang)], axis=-1),
                 jnp.concatenate([-jnp.sin(ang), jnp.sin(ang)], axis=-1))
    w_route = jnp.concatenate(
        list(_split3(jnp.concatenate([w_group, w_router], axis=-1).astype(F32)))
        + [jnp.zeros((depth, d, LANES - 3 * ROUTE_COLS), BF16)], axis=-1)
    b_route = jnp.concatenate(
        [b_group, b_router, jnp.zeros((depth, LANES - ROUTE_COLS), F32)], axis=-1)
    bf = lambda w: w.astype(BF16)

    cond = jnp.zeros((COND_ROWS, d), F32).at[:bs].set(c).at[bs].set(c_ctx)
    mod = _adaln_all(cond, w_ada, b_ada)

    x, h = _norm0(x_prompt.reshape(bp * tp, d), x_sample.reshape(bs * ts, d), g_norm1, mod, rows, 0)
    new_ret, new_hg, new_rg = [], [], []
    y_mix = jnp.zeros((rows.n, d), BF16)
    for l in range(depth):
        if l % 2 == 0:
            e = l // 2
            proj = _in_proj(h, w_even_in, e)
            y_mix, sr = _retention(proj, y_mix, log_gamma[e], g_ret_head[e], 0, bp, tp, None, None, True)
            y_mix, _ = _retention(proj, y_mix, log_gamma[e], g_ret_head[e], rows.n_ctx, bs, ts, rope_tabs,
                                  state_ret[:, e], False)
            y_mix, sh = _hgrn2(proj, y_mix, lb_all[e], g_hg_head[e], 0, bp, tp, None, True)
            y_mix, _ = _hgrn2(proj, y_mix, lb_all[e], g_hg_head[e], rows.n_ctx, bs, ts, state_hgrn[:, e], False)
            new_ret.append(sr)
            new_hg.append(sh)
            w_out = bf(w_even_out[e])
        else:
            o = l // 2
            proj = _in_proj(h, w_odd_in, o)
            gate_args = (conv_w[o], conv_b[o], bf(0.5 * w_a[o]), bf(0.5 * w_x[o]), 0.5 * b_a[o], 0.5 * b_x[o],
                         (-0.5 * RG_C) * softplus_neg_lam[o])
            y_mix, sg = _rglru(proj, y_mix, *gate_args, 0, bp, tp, None, True)
            y_mix, _ = _rglru(proj, y_mix, *gate_args, rows.n_ctx, bs, ts, state_rglru[:, o], False)
            new_rg.append(sg)
            w_out = bf(w_odd_out[o])
        x, h2, ids, wts = _outproj_route(y_mix, w_out, x, mod, g_norm2, w_route[l], b_route[l:l + 1], rows, l)
        tile_expert, n_used, src_rows, dest = _plan_routing(ids)
        ys = _moe_experts(h2, tile_expert, n_used, src_rows, w1, w3, w2, l)
        final = l == depth - 1
        x, h = _combine(ys, dest, wts, x, mod, g_final if final else g_norm1, rows, l, final)

    return (x.reshape(bp, tp, d), h.reshape(bs, ts, d), jnp.stack(new_ret, axis=1),
            jnp.stack(new_hg, axis=1), jnp.stack(new_rg, axis=1))
```
